```python
import math
import jax, jax.numpy as jnp
from jax import lax
import numpy as np

D_MODEL = 4096
BATCH = 2
SEQ = 4096
DEPTH = 1
DEC_BATCH = 4
DEC_SEQ = 4096
PAST_LEN = 128

HEAD_DIM = 128
N_META = 16
GRID_W = 64
Q_BLOCK = 128
A_HEADS = 8
A_WIDTH = A_HEADS * 2 * HEAD_DIM
B_HEADS = 16
B_KV_HEADS = 4
B_GROUP = B_HEADS // B_KV_HEADS
B_WIDTH = B_HEADS * HEAD_DIM
B_KV_WIDTH = B_KV_HEADS * HEAD_DIM
MIX_WIDTH = A_WIDTH + B_WIDTH
IN_COLS = 3 * A_WIDTH + B_WIDTH + 2 * B_KV_WIDTH
IN_SPLITS = (A_WIDTH, 2 * A_WIDTH, 3 * A_WIDTH, 3 * A_WIDTH + B_WIDTH, 3 * A_WIDTH + B_WIDTH + B_KV_WIDTH)
ROPE_THETA = 10000.0
ROPE_AXIS_DIM = HEAD_DIM // 2
REL_BUCKETS = 32
REL_MAX_DIST = 128
N_EXPERTS = 32
TOP_K = 4
D_FF = D_MODEL
SWIGLU_ALPHA = 1.702
SWIGLU_LIMIT = 7.0
MOE_BLOCK = 128
DEEPNORM_ALPHA = (2 * DEPTH) ** 0.25
DEEPNORM_BETA = (8 * DEPTH) ** -0.25
LN_EPS = 1e-5
RMS_EPS = 1e-6

kernel_name = "hymba_diffattn_axialgqa_moe_encoder"


def layer_norm(x, g, b):
    xf = x.astype(jnp.float32)
    mu = jnp.mean(xf, -1, keepdims=True)
    var = jnp.mean(jnp.square(xf - mu), -1, keepdims=True)
    return ((xf - mu) * lax.rsqrt(var + LN_EPS)).astype(x.dtype) * g + b


def rms_norm(x, g):
    xf = x.astype(jnp.float32)
    return (xf * lax.rsqrt(jnp.mean(xf * xf, -1, keepdims=True) + RMS_EPS)).astype(x.dtype) * g


def rel_bucket(rel):
    nb = REL_BUCKETS // 2
    max_exact = nb // 2
    ret = jnp.where(rel > 0, nb, 0)
    n = jnp.abs(rel)
    nf = jnp.maximum(n, 1).astype(jnp.float32)
    large = max_exact + (jnp.log(nf / max_exact) / math.log(REL_MAX_DIST / max_exact)
                         * (nb - max_exact)).astype(jnp.int32)
    large = jnp.minimum(large, nb - 1)
    return ret + jnp.where(n < max_exact, n, large)


def axial_rope_tables(n_tok):
    rows = n_tok // GRID_W
    row_id = jnp.repeat(jnp.arange(rows, dtype=jnp.int32), GRID_W)
    col_id = jnp.tile(jnp.arange(GRID_W, dtype=jnp.int32), rows)
    meta_id = jnp.zeros((N_META,), jnp.int32)
    row_id = jnp.concatenate([meta_id, row_id]).astype(jnp.float32)
    col_id = jnp.concatenate([meta_id, col_id]).astype(jnp.float32)
    inv = ROPE_THETA ** (-jnp.arange(0, ROPE_AXIS_DIM, 2, dtype=jnp.float32) / ROPE_AXIS_DIM)
    ang_r = row_id[:, None] * inv[None, :]
    ang_c = col_id[:, None] * inv[None, :]
    return jnp.cos(ang_r), jnp.sin(ang_r), jnp.cos(ang_c), jnp.sin(ang_c)


def rotate_half_axis(x, cos, sin):
    x1, x2 = jnp.split(x, 2, axis=-1)
    c = cos[:, None, :].astype(x.dtype)
    s = sin[:, None, :].astype(x.dtype)
    return jnp.concatenate([x1 * c - x2 * s, x2 * c + x1 * s], axis=-1)


def apply_axial_rope(x, cos_r, sin_r, cos_c, sin_c):
    x_row, x_col = jnp.split(x, 2, axis=-1)
    return jnp.concatenate([rotate_half_axis(x_row, cos_r, sin_r),
                            rotate_half_axis(x_col, cos_c, sin_c)], axis=-1)


def sweep_queries(block_fn, q):
    b, length = q.shape[0], q.shape[1]
    n_tok = length - N_META
    n_blk = n_tok // Q_BLOCK
    meta_out = block_fn(q[:, :N_META], jnp.arange(N_META, dtype=jnp.int32))
    q_blocks = jnp.moveaxis(q[:, N_META:].reshape((b, n_blk, Q_BLOCK) + q.shape[2:]), 1, 0)
    pos = (N_META + jnp.arange(n_tok, dtype=jnp.int32)).reshape(n_blk, Q_BLOCK)
    out = lax.map(lambda qp: block_fn(qp[0], qp[1]), (q_blocks, pos))
    out = jnp.moveaxis(out, 0, 1).reshape((b, n_tok) + out.shape[3:])
    return jnp.concatenate([meta_out, out], axis=1)


def diff_attn_block(q_blk, q_pos, k, v, rel_bias, lam):
    n_keys = k.shape[1]
    s = jnp.einsum('bqmhd,bkmhd->bmhqk', q_blk, k).astype(jnp.float32) * (HEAD_DIM ** -0.5)
    rel = jnp.arange(n_keys, dtype=jnp.int32)[None, :] - q_pos[:, None]
    bias = jnp.transpose(rel_bias[rel_bucket(rel)], (2, 0, 1)).astype(jnp.float32)
    p = jax.nn.softmax(s + bias[None, None], axis=-1)
    w = p[:, 0] - lam * p[:, 1]
    return jnp.einsum('bhqk,bkhe->bqhe', w.astype(v.dtype), v)


def gqa_block(q_blk, k, v):
    s = jnp.einsum('bqngd,bknd->bngqk', q_blk, k).astype(jnp.float32) * (HEAD_DIM ** -0.5)
    p = jax.nn.softmax(s, axis=-1)
    return jnp.einsum('bngqk,bknd->bqngd', p.astype(v.dtype), v)


def token_mixers(h, l, p):
    b, length, _ = h.shape
    n_tok = length - N_META
    proj = h @ p['w_in'][l]
    a_q, a_k, a_v, b_q, b_k, b_v = jnp.split(proj, IN_SPLITS, axis=-1)

    lam_init = 0.8 - 0.6 * math.exp(-0.3 * l)
    lam = (jnp.exp(jnp.sum(p['lambda_q1'][l].astype(jnp.float32) * p['lambda_k1'][l].astype(jnp.float32)))
           - jnp.exp(jnp.sum(p['lambda_q2'][l].astype(jnp.float32) * p['lambda_k2'][l].astype(jnp.float32)))
           + lam_init)
    a_q = a_q.reshape(b, length, 2, A_HEADS, HEAD_DIM)
    a_k = a_k.reshape(b, length, 2, A_HEADS, HEAD_DIM)
    a_v = a_v.reshape(b, length, A_HEADS, 2 * HEAD_DIM)
    rel_bias = p['rel_bias']
    a_out = sweep_queries(lambda qb, qpos: diff_attn_block(qb, qpos, a_k, a_v, rel_bias, lam), a_q)
    a_out = (rms_norm(a_out, p['a_subln'][l]) * (1.0 - lam_init)).reshape(b, length, A_WIDTH)

    cos_r, sin_r, cos_c, sin_c = axial_rope_tables(n_tok)
    b_q = rms_norm(b_q.reshape(b, length, B_HEADS, HEAD_DIM), p['q_norm'][l])
    b_k = rms_norm(b_k.reshape(b, length, B_KV_HEADS, HEAD_DIM), p['k_norm'][l])
    b_q = apply_axial_rope(b_q, cos_r, sin_r, cos_c, sin_c).reshape(b, length, B_KV_HEADS, B_GROUP, HEAD_DIM)
    b_k = apply_axial_rope(b_k, cos_r, sin_r, cos_c, sin_c)
    b_v = b_v.reshape(b, length, B_KV_HEADS, HEAD_DIM)
    b_out = sweep_queries(lambda qb, qpos: gqa_block(qb, b_k, b_v), b_q).reshape(b, length, B_WIDTH)

    return jnp.concatenate([a_out, b_out], axis=-1) @ p['w_out'][l]


def moe(h, l, p):
    b, length, d = h.shape
    xt = h.reshape(-1, d)
    n = xt.shape[0]
    w1, b1, w2, b2 = p['w1'][l], p['b1'][l], p['w2'][l], p['b2'][l]
    logits = (xt @ p['w_router'][l] + p['b_router'][l]).astype(jnp.float32)
    top_val, top_idx = lax.top_k(logits, TOP_K)
    gate = jax.nn.softmax(top_val, axis=-1)

    n_assign = n * TOP_K
    e_flat = top_idx.reshape(-1)
    g_flat = gate.reshape(-1)
    tok_flat = jnp.arange(n_assign, dtype=jnp.int32) // TOP_K
    order = jnp.argsort(e_flat)
    e_sorted = e_flat[order]
    counts = jnp.zeros((N_EXPERTS,), jnp.int32).at[e_flat].add(1)
    padded = (counts + MOE_BLOCK - 1) // MOE_BLOCK * MOE_BLOCK
    start = jnp.cumsum(counts) - counts
    pend = jnp.cumsum(padded)
    pstart = pend - padded
    dest = pstart[e_sorted] + (jnp.arange(n_assign, dtype=jnp.int32) - start[e_sorted])
    n_blocks = (n_assign + N_EXPERTS * (MOE_BLOCK - 1)) // MOE_BLOCK + 1
    n_rows = n_blocks * MOE_BLOCK
    row_tok = jnp.zeros((n_rows,), jnp.int32).at[dest].set(tok_flat[order])
    row_gate = jnp.zeros((n_rows,), jnp.float32).at[dest].set(g_flat[order])
    block_start = jnp.arange(n_blocks, dtype=jnp.int32) * MOE_BLOCK
    block_exp = jnp.minimum(jnp.searchsorted(pend, block_start, side='right'), N_EXPERTS - 1)

    def expert_block(args):
        tok, g, e = args
        xb = xt[tok]
        hgu = xb @ w1[e] + b1[e]
        x_glu, x_lin = jnp.split(hgu, 2, axis=-1)
        x_glu = jnp.minimum(x_glu, SWIGLU_LIMIT)
        x_lin = jnp.clip(x_lin, -SWIGLU_LIMIT, SWIGLU_LIMIT)
        act = x_glu * jax.nn.sigmoid(SWIGLU_ALPHA * x_glu) * (x_lin + 1.0)
        yb = act @ w2[e] + b2[e]
        return yb * g[:, None].astype(yb.dtype)

    rows = lax.map(expert_block, (row_tok.reshape(n_blocks, MOE_BLOCK),
                                  row_gate.reshape(n_blocks, MOE_BLOCK), block_exp))
    y = jnp.zeros_like(xt).at[row_tok].add(rows.reshape(n_rows, d))
    return y.reshape(b, length, d)


def encoder_trunk(x, p):
    b = x.shape[0]
    meta = jnp.broadcast_to(p['meta_tokens'][None].astype(x.dtype), (b, N_META, D_MODEL))
    h = jnp.concatenate([meta, x], axis=1)
    for l in range(DEPTH):
        h = layer_norm(DEEPNORM_ALPHA * h + token_mixers(h, l, p), p['ln1_g'][l], p['ln1_b'][l])
        h = layer_norm(DEEPNORM_ALPHA * h + moe(h, l, p), p['ln2_g'][l], p['ln2_b'][l])
    return h[:, N_META:]


def setup_inputs(seed: int = 0) -> dict:
    key = jax.random.key(seed)
    ks = jax.random.split(key, 24)
    f32 = jnp.float32

    def nrm(k, shape, scale):
        return jax.random.normal(k, shape, f32) * scale

    col_scale = jnp.concatenate([
        jnp.ones((2 * A_WIDTH,), f32), jnp.full((A_WIDTH,), DEEPNORM_BETA, f32),
        jnp.ones((B_WIDTH + B_KV_WIDTH,), f32), jnp.full((B_KV_WIDTH,), DEEPNORM_BETA, f32)])
    return {
        'x_prompt': nrm(ks[0], (BATCH, SEQ, D_MODEL), 1.0),
        'x_sample': nrm(ks[1], (DEC_BATCH, DEC_SEQ, D_MODEL), 1.0),
        'meta_tokens': nrm(ks[2], (N_META, D_MODEL), 1.0),
        'rel_bias': nrm(ks[3], (REL_BUCKETS, A_HEADS), 0.5),
        'w_in': nrm(ks[4], (DEPTH, D_MODEL, IN_COLS), D_MODEL ** -0.5) * col_scale,
        'lambda_q1': nrm(ks[5], (DEPTH, HEAD_DIM), 0.1),
        'lambda_k1': nrm(ks[6], (DEPTH, HEAD_DIM), 0.1),
        'lambda_q2': nrm(ks[7], (DEPTH, HEAD_DIM), 0.1),
        'lambda_k2': nrm(ks[8], (DEPTH, HEAD_DIM), 0.1),
        'a_subln': 1.0 + nrm(ks[9], (DEPTH, 2 * HEAD_DIM), 0.02),
        'q_norm': 1.0 + nrm(ks[10], (DEPTH, HEAD_DIM), 0.02),
        'k_norm': 1.0 + nrm(ks[11], (DEPTH, HEAD_DIM), 0.02),
        'w_out': nrm(ks[12], (DEPTH, MIX_WIDTH, D_MODEL), MIX_WIDTH ** -0.5 * DEEPNORM_BETA),
        'ln1_g': 1.0 + nrm(ks[13], (DEPTH, D_MODEL), 0.02),
        'ln1_b': nrm(ks[14], (DEPTH, D_MODEL), 0.02),
        'w_router': nrm(ks[15], (DEPTH, D_MODEL, N_EXPERTS), D_MODEL ** -0.5),
        'b_router': nrm(ks[16], (DEPTH, N_EXPERTS), 0.01),
        'w1': nrm(ks[17], (DEPTH, N_EXPERTS, D_MODEL, 2 * D_FF), D_MODEL ** -0.5),
        'b1': nrm(ks[18], (DEPTH, N_EXPERTS, 2 * D_FF), 0.01),
        'w2': nrm(ks[19], (DEPTH, N_EXPERTS, D_FF, D_MODEL), D_FF ** -0.5 * DEEPNORM_BETA),
        'b2': nrm(ks[20], (DEPTH, N_EXPERTS, D_MODEL), 0.01),
        'ln2_g': 1.0 + nrm(ks[21], (DEPTH, D_MODEL), 0.02),
        'ln2_b': nrm(ks[22], (DEPTH, D_MODEL), 0.02),
    }


def reference(x_prompt, x_sample, meta_tokens, rel_bias, w_in, lambda_q1, lambda_k1, lambda_q2,
              lambda_k2, a_subln, q_norm, k_norm, w_out, ln1_g, ln1_b, w_router, b_router,
              w1, b1, w2, b2, ln2_g, ln2_b):
    params = {
        'meta_tokens': meta_tokens, 'rel_bias': rel_bias, 'w_in': w_in,
        'lambda_q1': lambda_q1, 'lambda_k1': lambda_k1, 'lambda_q2': lambda_q2, 'lambda_k2': lambda_k2,
        'a_subln': a_subln, 'q_norm': q_norm, 'k_norm': k_norm, 'w_out': w_out,
        'ln1_g': ln1_g, 'ln1_b': ln1_b, 'w_router': w_router, 'b_router': b_router,
        'w1': w1, 'b1': b1, 'w2': w2, 'b2': b2, 'ln2_g': ln2_g, 'ln2_b': ln2_b,
    }
    y_prompt = encoder_trunk(x_prompt, params)
    y_sample = encoder_trunk(x_sample, params)
    return (y_prompt, y_sample)
```

```python
import functools
import math

import jax
import jax.numpy as jnp
from jax import lax
from jax.experimental import pallas as pl
from jax.experimental.pallas import tpu as pltpu

HEAD_DIM = 128
N_META = 16
GRID_W = 64
A_HEADS = 8
A_WIDTH = A_HEADS * 2 * HEAD_DIM
B_HEADS = 16
B_KV_HEADS = 4
B_GROUP = B_HEADS // B_KV_HEADS
B_WIDTH = B_HEADS * HEAD_DIM
B_KV_WIDTH = B_KV_HEADS * HEAD_DIM
IN_COLS = 3 * A_WIDTH + B_WIDTH + 2 * B_KV_WIDTH
ROPE_THETA = 10000.0
ROPE_AXIS_DIM = HEAD_DIM // 2
REL_BUCKETS = 32
REL_MAX_DIST = 128
N_EXPERTS = 32
TOP_K = 4
SWIGLU_ALPHA = 1.702
SWIGLU_LIMIT = 7.0
DEPTH = 1
DEEPNORM_ALPHA = (2 * DEPTH) ** 0.25
LN_EPS = 1e-5
RMS_EPS = 1e-6
LAMBDA_INIT = 0.8 - 0.6 * math.exp(-0.3 * 0)
ATTN_SCALE = HEAD_DIM ** -0.5

LANES = 128
META_PAD = 128
MASK_VALUE = -1e30
VMEM_LIMIT = 56 * 1024 * 1024

BF16 = jnp.bfloat16
F32 = jnp.float32


def _cparams(sem):
    return pltpu.CompilerParams(dimension_semantics=sem, vmem_limit_bytes=VMEM_LIMIT)


def _proj_kernel(a_ref, b_ref, s_ref, o_ref):
    acc = jnp.dot(a_ref[...], b_ref[...], preferred_element_type=F32)
    o_ref[...] = (acc * s_ref[...]).astype(o_ref.dtype)


def _projection(a, b, colscale, tm, tn):
    m, k = a.shape
    n = b.shape[1]
    tm = min(tm, m)
    return pl.pallas_call(
        _proj_kernel,
        grid=(m // tm, n // tn),
        in_specs=[
            pl.BlockSpec((tm, k), lambda i, j: (i, 0)),
            pl.BlockSpec((k, tn), lambda i, j: (0, j)),
            pl.BlockSpec((1, tn), lambda i, j: (0, j)),
        ],
        out_specs=pl.BlockSpec((tm, tn), lambda i, j: (i, j)),
        out_shape=jax.ShapeDtypeStruct((m, n), BF16),
        compiler_params=_cparams(("parallel", "parallel")),
        name="in_proj",
    )(a, b, colscale)


def _qk_prep_kernel(x_ref, g_ref, c_ref, s_ref, o_ref):
    cos = c_ref[...]
    sin = s_ref[...]
    lane = lax.broadcasted_iota(jnp.int32, cos.shape, 1)
    first_half = (lane % (ROPE_AXIS_DIM)) < (ROPE_AXIS_DIM // 2)
    for hh in range(x_ref.shape[1] // HEAD_DIM):
        sl = slice(hh * HEAD_DIM, (hh + 1) * HEAD_DIM)
        x = x_ref[:, sl].astype(F32)
        ms = jnp.mean(x * x, axis=-1, keepdims=True)
        y = x * lax.rsqrt(ms + RMS_EPS) * g_ref[:, sl]
        partner = jnp.where(first_half,
                            pltpu.roll(y, HEAD_DIM - ROPE_AXIS_DIM // 2, 1),
                            pltpu.roll(y, ROPE_AXIS_DIM // 2, 1))
        o_ref[:, sl] = (y * cos + partner * sin).astype(o_ref.dtype)


def _qk_prep(proj, gains, cos, sin, tm):
    m = proj.shape[0]
    tm = min(tm, m)
    w = B_KV_WIDTH
    first = (3 * A_WIDTH) // w
    nblk = (B_WIDTH + B_KV_WIDTH) // w
    ntab = cos.shape[0] // tm
    return pl.pallas_call(
        _qk_prep_kernel,
        grid=(m // tm, nblk),
        in_specs=[
            pl.BlockSpec((tm, w), lambda i, j: (i, first + j)),
            pl.BlockSpec((None, 1, w), lambda i, j: (j, 0, 0)),
            pl.BlockSpec((tm, HEAD_DIM), lambda i, j: (i % ntab, 0)),
            pl.BlockSpec((tm, HEAD_DIM), lambda i, j: (i % ntab, 0)),
        ],
        out_specs=pl.BlockSpec((tm, w), lambda i, j: (i, j)),
        out_shape=jax.ShapeDtypeStruct((m, B_WIDTH + B_KV_WIDTH), BF16),
        compiler_params=_cparams(("parallel", "parallel")),
        name="qk_prep",
    )(proj, gains, cos, sin)


def _scores(q, k):
    return lax.dot_general(q, k, (((1,), (1,)), ((), ())), preferred_element_type=F32)


def _softmax_init(s, v, m_ref, l_ref, acc_ref):
    m = jnp.max(s, axis=-1, keepdims=True)
    p = jnp.exp(s - m)
    m_ref[...] = m
    l_ref[...] = jnp.sum(p, axis=-1, keepdims=True)
    acc_ref[...] = jnp.dot(p.astype(v.dtype), v, preferred_element_type=F32)


def _softmax_update(s, shift, v, m_ref, l_ref, acc_ref):
    m_prev = m_ref[...]
    m_new = jnp.maximum(m_prev, jnp.max(s, axis=-1, keepdims=True) + shift)
    p = jnp.exp(s - (m_new - shift))
    alpha = jnp.exp(m_prev - m_new)
    l_ref[...] = alpha * l_ref[...] + jnp.sum(p, axis=-1, keepdims=True)
    acc_ref[...] = alpha * acc_ref[...] + jnp.dot(p.astype(v.dtype), v, preferred_element_type=F32)
    m_ref[...] = m_new


def _diff_attn_kernel(c_ref, q1_ref, q2_ref, k1_ref, k2_ref, v_ref, mk1_ref, mk2_ref, mv_ref,
                      band_ref, mbias_ref, g_ref, o_ref, m_ref, l_ref, acc_ref, *, tk):
    h = pl.program_id(1)
    i = pl.program_id(2)
    nk = k1_ref.shape[0] // tk
    bias_left = c_ref[h]
    bias_right = c_ref[A_HEADS + h]
    lam = c_ref[2 * A_HEADS]
    qs = (q1_ref[...], q2_ref[...])
    krefs = (k1_ref, k2_ref)

    meta_ok = lax.broadcasted_iota(jnp.int32, mbias_ref.shape, 1) < N_META
    mv = mv_ref[...]
    for a, mk_ref in enumerate((mk1_ref, mk2_ref)):
        s = _scores(qs[a], mk_ref[...]) + mbias_ref[...]
        s = jnp.where(meta_ok, s, MASK_VALUE)
        _softmax_init(s, mv, m_ref.at[a], l_ref.at[a], acc_ref.at[a])

    def chunk(j):
        return pl.ds(pl.multiple_of(j * tk, tk), tk)

    def const_step(shift):
        def body(j, carry):
            v = v_ref[chunk(j), :]
            for a in range(2):
                s = _scores(qs[a], krefs[a][chunk(j), :])
                _softmax_update(s, shift, v, m_ref.at[a], l_ref.at[a], acc_ref.at[a])
            return carry
        return body

    def band_step(j, d):
        v = v_ref[chunk(j), :]
        for a in range(2):
            s = _scores(qs[a], krefs[a][chunk(j), :]) + band_ref[d]
            _softmax_update(s, 0.0, v, m_ref.at[a], l_ref.at[a], acc_ref.at[a])

    lax.fori_loop(0, jnp.maximum(i - 1, 0), const_step(bias_left), 0)

    @pl.when(i >= 1)
    def _():
        band_step(i - 1, 0)

    band_step(i, 1)

    @pl.when(i + 1 < nk)
    def _():
        band_step(i + 1, 2)

    lax.fori_loop(jnp.minimum(i + 2, nk), nk, const_step(bias_right), 0)

    o = acc_ref[0] / l_ref[0] - lam * (acc_ref[1] / l_ref[1])
    ms = jnp.mean(o * o, axis=-1, keepdims=True)
    o_ref[...] = (o * lax.rsqrt(ms + RMS_EPS) * g_ref[...] * (1.0 - LAMBDA_INIT)).astype(o_ref.dtype)


def _diff_attention(proj3, meta_proj, band, meta_bias, consts, subln, tq):
    nb, seq, _ = proj3.shape
    dv = 2 * HEAD_DIM
    kq2 = A_HEADS
    kk1 = A_WIDTH // HEAD_DIM
    kk2 = kk1 + A_HEADS
    kv = (2 * A_WIDTH) // dv
    return pl.pallas_call(
        functools.partial(_diff_attn_kernel, tk=tq),
        grid=(nb, A_HEADS, seq // tq),
        in_specs=[
            pl.BlockSpec(memory_space=pltpu.SMEM),
            pl.BlockSpec((None, tq, HEAD_DIM), lambda b, h, i: (b, i, h)),
            pl.BlockSpec((None, tq, HEAD_DIM), lambda b, h, i: (b, i, kq2 + h)),
            pl.BlockSpec((None, seq, HEAD_DIM), lambda b, h, i: (b, 0, kk1 + h)),
            pl.BlockSpec((None, seq, HEAD_DIM), lambda b, h, i: (b, 0, kk2 + h)),
            pl.BlockSpec((None, seq, dv), lambda b, h, i: (b, 0, kv + h)),
            pl.BlockSpec((META_PAD, HEAD_DIM), lambda b, h, i: (0, kk1 + h)),
            pl.BlockSpec((META_PAD, HEAD_DIM), lambda b, h, i: (0, kk2 + h)),
            pl.BlockSpec((META_PAD, dv), lambda b, h, i: (0, kv + h)),
            pl.BlockSpec((None, 3, tq, tq), lambda b, h, i: (h, 0, 0, 0)),
            pl.BlockSpec((None, tq, META_PAD), lambda b, h, i: (h, i, 0)),
            pl.BlockSpec((1, dv), lambda b, h, i: (0, 0)),
        ],
        out_specs=pl.BlockSpec((None, tq, dv), lambda b, h, i: (b, i, h)),
        out_shape=jax.ShapeDtypeStruct((nb, seq, A_WIDTH), BF16),
        scratch_shapes=[
            pltpu.VMEM((2, tq, 1), F32),
            pltpu.VMEM((2, tq, 1), F32),
            pltpu.VMEM((2, tq, dv), F32),
        ],
        compiler_params=_cparams(("parallel", "parallel", "parallel")),
        name="diff_attn",
    )(consts, proj3, proj3, proj3, proj3, proj3, meta_proj, meta_proj, meta_proj,
      band, meta_bias, subln)


def _gqa_kernel(q_ref, k_ref, v_ref, mk_ref, mv_ref, o_ref, m_ref, l_ref, acc_ref, *, tk):
    nk = k_ref.shape[0] // tk
    q = q_ref[...]
    s = _scores(q, mk_ref[...])
    meta_ok = lax.broadcasted_iota(jnp.int32, s.shape, 1) < N_META
    s = jnp.where(meta_ok, s, MASK_VALUE)
    _softmax_init(s, mv_ref[...], m_ref, l_ref, acc_ref)

    def body(j, carry):
        rows = pl.ds(pl.multiple_of(j * tk, tk), tk)
        _softmax_update(_scores(q, k_ref[rows, :]), 0.0, v_ref[rows, :], m_ref, l_ref, acc_ref)
        return carry

    lax.fori_loop(0, nk, body, 0)
    o_ref[...] = (acc_ref[...] / l_ref[...]).astype(o_ref.dtype)


def _gqa_attention(bqk3, proj3, meta_bqk, meta_proj, tq, tk):
    nb, seq, _ = bqk3.shape
    kk = B_HEADS
    kv = (3 * A_WIDTH + B_WIDTH + B_KV_WIDTH) // HEAD_DIM
    return pl.pallas_call(
        functools.partial(_gqa_kernel, tk=tk),
        grid=(nb, B_HEADS, seq // tq),
        in_specs=[
            pl.BlockSpec((None, tq, HEAD_DIM), lambda b, h, i: (b, i, h)),
            pl.BlockSpec((None, seq, HEAD_DIM), lambda b, h, i: (b, 0, kk + h // B_GROUP)),
            pl.BlockSpec((None, seq, HEAD_DIM), lambda b, h, i: (b, 0, kv + h // B_GROUP)),
            pl.BlockSpec((META_PAD, HEAD_DIM), lambda b, h, i: (0, kk + h // B_GROUP)),
            pl.BlockSpec((META_PAD, HEAD_DIM), lambda b, h, i: (0, kv + h // B_GROUP)),
        ],
        out_specs=pl.BlockSpec((None, tq, HEAD_DIM), lambda b, h, i: (b, i, h)),
        out_shape=jax.ShapeDtypeStruct((nb, seq, B_WIDTH), BF16),
        scratch_shapes=[
            pltpu.VMEM((tq, 1), F32),
            pltpu.VMEM((tq, 1), F32),
            pltpu.VMEM((tq, HEAD_DIM), F32),
        ],
        compiler_params=_cparams(("parallel", "parallel", "parallel")),
        name="gqa_attn",
    )(bqk3, bqk3, proj3, meta_bqk, meta_proj)


def _layer_norm(y, g, b):
    mu = jnp.mean(y, axis=-1, keepdims=True)
    yc = y - mu
    var = jnp.mean(yc * yc, axis=-1, keepdims=True)
    return yc * lax.rsqrt(var + LN_EPS) * g + b


def _out_proj_ln_kernel(a_ref, b_ref, w_ref, x_ref, g_ref, beta_ref, h_ref, hb_ref, acc_ref, *, ka):
    k = pl.program_id(1)

    @pl.when(k == 0)
    def _():
        acc_ref[...] = jnp.zeros_like(acc_ref)

    @pl.when(k < ka)
    def _():
        acc_ref[...] += jnp.dot(a_ref[...], w_ref[...], preferred_element_type=F32)

    @pl.when(k >= ka)
    def _():
        acc_ref[...] += jnp.dot(b_ref[...], w_ref[...], preferred_element_type=F32)

    @pl.when(k == pl.num_programs(1) - 1)
    def _():
        h = _layer_norm(DEEPNORM_ALPHA * x_ref[...] + acc_ref[...], g_ref[...], beta_ref[...])
        h_ref[...] = h
        hb_ref[...] = h.astype(hb_ref.dtype)


def _out_proj_ln(a_out, b_out, w_out, x, g, beta, tm, tk):
    m, d = x.shape
    ka = a_out.shape[1] // tk
    kb = b_out.shape[1] // tk
    return pl.pallas_call(
        functools.partial(_out_proj_ln_kernel, ka=ka),
        grid=(m // tm, ka + kb),
        in_specs=[
            pl.BlockSpec((tm, tk), lambda i, k: (i, jnp.minimum(k, ka - 1))),
            pl.BlockSpec((tm, tk), lambda i, k: (i, jnp.maximum(k - ka, 0))),
            pl.BlockSpec((tk, d), lambda i, k: (k, 0)),
            pl.BlockSpec((tm, d), lambda i, k: (i, 0)),
            pl.BlockSpec((1, d), lambda i, k: (0, 0)),
            pl.BlockSpec((1, d), lambda i, k: (0, 0)),
        ],
        out_specs=[
            pl.BlockSpec((tm, d), lambda i, k: (i, 0)),
            pl.BlockSpec((tm, d), lambda i, k: (i, 0)),
        ],
        out_shape=[jax.ShapeDtypeStruct((m, d), F32), jax.ShapeDtypeStruct((m, d), BF16)],
        scratch_shapes=[pltpu.VMEM((tm, d), F32)],
        compiler_params=_cparams(("parallel", "arbitrary")),
        name="out_proj_ln1",
    )(a_out, b_out, w_out, x, g, beta)


def _router_kernel(h_ref, w_ref, b_ref, o_ref):
    o_ref[...] = jnp.dot(h_ref[...], w_ref[...], precision=lax.Precision.HIGHEST,
                         preferred_element_type=F32) + b_ref[...]


def _router(h, w, b, tm):
    m, d = h.shape
    e = w.shape[1]
    return pl.pallas_call(
        _router_kernel,
        grid=(m // tm,),
        in_specs=[
            pl.BlockSpec((tm, d), lambda i: (i, 0)),
            pl.BlockSpec((d, e), lambda i: (0, 0)),
            pl.BlockSpec((1, e), lambda i: (0, 0)),
        ],
        out_specs=pl.BlockSpec((tm, e), lambda i: (i, 0)),
        out_shape=jax.ShapeDtypeStruct((m, e), F32),
        compiler_params=_cparams(("parallel",)),
        name="router",
    )(h, w, b)


def _ffn1_kernel(te_ref, nv_ref, x_ref, wg_ref, wl_ref, bg_ref, bl_ref, o_ref):
    @pl.when(pl.program_id(0) < nv_ref[0])
    def _():
        x = x_ref[...]
        hg = jnp.dot(x, wg_ref[...].astype(BF16), preferred_element_type=F32) + bg_ref[...]
        hl = jnp.dot(x, wl_ref[...].astype(BF16), preferred_element_type=F32) + bl_ref[...]
        hg = jnp.minimum(hg, SWIGLU_LIMIT)
        hl = jnp.clip(hl, -SWIGLU_LIMIT, SWIGLU_LIMIT)
        act = hg * (1.0 / (1.0 + jnp.exp(-SWIGLU_ALPHA * hg))) * (hl + 1.0)
        o_ref[...] = act.astype(o_ref.dtype)


def _ffn2_kernel(te_ref, nv_ref, a_ref, w_ref, b_ref, g_ref, o_ref):
    @pl.when(pl.program_id(0) < nv_ref[0])
    def _():
        y = jnp.dot(a_ref[...], w_ref[...].astype(BF16), preferred_element_type=F32) + b_ref[...]
        o_ref[...] = y * g_ref[...]


def _grouped_maps(ncol):
    def row(t, j, te, nv):
        return jnp.minimum(t, nv[0] - 1)

    def col(t, j, te, nv):
        return jnp.where(t < nv[0], j, ncol - 1)

    def expert(t, j, te, nv):
        return te[row(t, j, te, nv)]

    return row, col, expert


def _expert_ffn1(x_rows, w1, b1, tile_expert, n_valid, tm, tn):
    n_rows, d = x_rows.shape
    dff = w1.shape[2] // 2
    ncol = dff // tn
    row, col, expert = _grouped_maps(ncol)
    grid_spec = pltpu.PrefetchScalarGridSpec(
        num_scalar_prefetch=2,
        grid=(n_rows // tm, ncol),
        in_specs=[
            pl.BlockSpec((tm, d), lambda *a: (row(*a), 0)),
            pl.BlockSpec((None, d, tn), lambda *a: (expert(*a), 0, col(*a))),
            pl.BlockSpec((None, d, tn), lambda *a: (expert(*a), 0, ncol + col(*a))),
            pl.BlockSpec((None, 1, tn), lambda *a: (expert(*a), 0, col(*a))),
            pl.BlockSpec((None, 1, tn), lambda *a: (expert(*a), 0, ncol + col(*a))),
        ],
        out_specs=pl.BlockSpec((tm, tn), lambda *a: (row(*a), col(*a))),
    )
    return pl.pallas_call(
        _ffn1_kernel,
        grid_spec=grid_spec,
        out_shape=jax.ShapeDtypeStruct((n_rows, dff), BF16),
        compiler_params=_cparams(("arbitrary", "arbitrary")),
        name="moe_ffn1",
    )(tile_expert, n_valid, x_rows, w1, w1, b1, b1)


def _expert_ffn2(act, w2, b2, row_gate, tile_expert, n_valid, tm, tn):
    n_rows, dff = act.shape
    d = w2.shape[2]
    ncol = d // tn
    row, col, expert = _grouped_maps(ncol)
    grid_spec = pltpu.PrefetchScalarGridSpec(
        num_scalar_prefetch=2,
        grid=(n_rows // tm, ncol),
        in_specs=[
            pl.BlockSpec((tm, dff), lambda *a: (row(*a), 0)),
            pl.BlockSpec((None, dff, tn), lambda *a: (expert(*a), 0, col(*a))),
            pl.BlockSpec((None, 1, tn), lambda *a: (expert(*a), 0, col(*a))),
            pl.BlockSpec((tm, 1), lambda *a: (row(*a), 0)),
        ],
        out_specs=pl.BlockSpec((tm, tn), lambda *a: (row(*a), col(*a))),
    )
    return pl.pallas_call(
        _ffn2_kernel,
        grid_spec=grid_spec,
        out_shape=jax.ShapeDtypeStruct((n_rows, d), F32),
        compiler_params=_cparams(("arbitrary", "arbitrary")),
        name="moe_ffn2",
    )(tile_expert, n_valid, act, w2, b2, row_gate)


def _residual_ln_kernel(h_ref, y_ref, g_ref, b_ref, o_ref):
    o_ref[...] = _layer_norm(DEEPNORM_ALPHA * h_ref[...] + y_ref[...], g_ref[...], b_ref[...])


def _residual_ln(h, y, g, b, tm):
    m, d = h.shape
    return pl.pallas_call(
        _residual_ln_kernel,
        grid=(m // tm,),
        in_specs=[
            pl.BlockSpec((tm, d), lambda i: (i, 0)),
            pl.BlockSpec((tm, d), lambda i: (i, 0)),
            pl.BlockSpec((1, d), lambda i: (0, 0)),
            pl.BlockSpec((1, d), lambda i: (0, 0)),
        ],
        out_specs=pl.BlockSpec((tm, d), lambda i: (i, 0)),
        out_shape=jax.ShapeDtypeStruct((m, d), F32),
        compiler_params=_cparams(("parallel",)),
        name="residual_ln2",
    )(h, y, g, b)


def _rel_bucket(rel):
    nb = REL_BUCKETS // 2
    max_exact = nb // 2
    ret = jnp.where(rel > 0, nb, 0)
    n = jnp.abs(rel)
    nf = jnp.maximum(n, 1).astype(F32)
    large = max_exact + (jnp.log(nf / max_exact) / math.log(REL_MAX_DIST / max_exact)
                         * (nb - max_exact)).astype(jnp.int32)
    large = jnp.minimum(large, nb - 1)
    return ret + jnp.where(n < max_exact, n, large)


def _bias_tables(rel_bias, seq, tq):
    def bias_of(rel):
        return rel_bias[_rel_bucket(rel)].astype(F32)

    r = jnp.arange(tq, dtype=jnp.int32)
    d = jnp.arange(-1, 2, dtype=jnp.int32)
    rel = d[:, None, None] * tq + r[None, None, :] - r[None, :, None]
    band = jnp.moveaxis(bias_of(rel), -1, 0)
    assert tq >= REL_MAX_DIST
    far = bias_of(jnp.array([-(tq + 1), tq + 1], jnp.int32))
    q_pos = N_META + jnp.arange(seq, dtype=jnp.int32)
    m_pos = jnp.arange(META_PAD, dtype=jnp.int32)
    meta = jnp.moveaxis(bias_of(m_pos[None, :] - q_pos[:, None]), -1, 0)
    return band, meta, far


def _rope_tables(seq):
    tok = jnp.arange(seq, dtype=jnp.int32)
    row_id = (tok // GRID_W).astype(F32)
    col_id = (tok % GRID_W).astype(F32)
    inv = ROPE_THETA ** (-jnp.arange(0, ROPE_AXIS_DIM, 2, dtype=F32) / ROPE_AXIS_DIM)
    ang_r = row_id[:, None] * inv[None, :]
    ang_c = col_id[:, None] * inv[None, :]
    cos = jnp.concatenate([jnp.cos(ang_r)] * 2 + [jnp.cos(ang_c)] * 2, axis=-1)
    sin = jnp.concatenate([-jnp.sin(ang_r), jnp.sin(ang_r), -jnp.sin(ang_c), jnp.sin(ang_c)], axis=-1)
    return cos, sin


def _route(logits, tm, n_tiles):
    n = logits.shape[0]
    top_val, top_idx = lax.top_k(logits, TOP_K)
    gate = jax.nn.softmax(top_val, axis=-1)
    n_assign = n * TOP_K
    e_flat = top_idx.reshape(-1)
    g_flat = gate.reshape(-1)
    counts = jnp.zeros((N_EXPERTS,), jnp.int32).at[e_flat].add(1)
    tiles_per = (counts + tm - 1) // tm
    tile_end = jnp.cumsum(tiles_per)
    pstart = (tile_end - tiles_per) * tm
    start = jnp.cumsum(counts) - counts
    order = jnp.argsort(e_flat)
    e_sorted = e_flat[order]
    dest_sorted = pstart[e_sorted] + (jnp.arange(n_assign, dtype=jnp.int32) - start[e_sorted])
    n_rows = n_tiles * tm
    row_tok = jnp.zeros((n_rows,), jnp.int32).at[dest_sorted].set((order // TOP_K).astype(jnp.int32))
    row_gate = jnp.zeros((n_rows,), F32).at[dest_sorted].set(g_flat[order])
    dest = jnp.zeros((n_assign,), jnp.int32).at[order].set(dest_sorted).reshape(n, TOP_K)
    tile_expert = jnp.minimum(
        jnp.searchsorted(tile_end, jnp.arange(n_tiles, dtype=jnp.int32), side='right'),
        N_EXPERTS - 1).astype(jnp.int32)
    n_valid = tile_end[-1:].astype(jnp.int32)
    return row_tok, row_gate, dest, tile_expert, n_valid


PROJ_TM, PROJ_TN = 1024, 1024
PREP_TM = 512
ATTN_TQ = 512
OUT_TM, OUT_TK = 256, 512
ROUTER_TM = 512
MOE_TM, MOE_TN = 1024, 256
LN_TM = 256


def kernel(x_prompt, x_sample, meta_tokens, rel_bias, w_in, lambda_q1, lambda_k1, lambda_q2,
           lambda_k2, a_subln, q_norm, k_norm, w_out, ln1_g, ln1_b, w_router, b_router,
           w1, b1, w2, b2, ln2_g, ln2_b):
    d = x_prompt.shape[-1]
    seq = x_prompt.shape[1]
    assert x_sample.shape[1] == seq and seq % GRID_W == 0
    nb_p, nb_s = x_prompt.shape[0], x_sample.shape[0]
    nb = nb_p + nb_s
    x = jnp.concatenate([x_prompt.reshape(-1, d), x_sample.reshape(-1, d)], axis=0)
    n = x.shape[0]
    tq = min(ATTN_TQ, seq)

    w_in_b = w_in[0].astype(BF16)
    colscale = jnp.concatenate([jnp.full((A_WIDTH,), ATTN_SCALE, F32),
                                jnp.ones((IN_COLS - A_WIDTH,), F32)])[None]
    tn = min(PROJ_TN, IN_COLS)
    proj = _projection(x.astype(BF16), w_in_b, colscale, PROJ_TM, tn)
    meta_rows = jnp.zeros((META_PAD, d), F32).at[:N_META].set(meta_tokens)
    meta_proj = _projection(meta_rows.astype(BF16), w_in_b, colscale, META_PAD, tn)

    gains = jnp.concatenate([jnp.tile(q_norm[0] * ATTN_SCALE, B_HEADS),
                             jnp.tile(k_norm[0], B_KV_HEADS)]).reshape(-1, 1, B_KV_WIDTH)
    cos, sin = _rope_tables(seq)
    bqk = _qk_prep(proj, gains, cos, sin, min(PREP_TM, seq))
    meta_bqk = _qk_prep(meta_proj, gains, jnp.ones((META_PAD, HEAD_DIM), F32),
                        jnp.zeros((META_PAD, HEAD_DIM), F32), META_PAD)

    lam = (jnp.exp(jnp.sum(lambda_q1[0].astype(F32) * lambda_k1[0].astype(F32)))
           - jnp.exp(jnp.sum(lambda_q2[0].astype(F32) * lambda_k2[0].astype(F32))) + LAMBDA_INIT)
    band, meta_bias, far = _bias_tables(rel_bias, seq, tq)
    consts = jnp.concatenate([far[0], far[1], lam[None]]).astype(F32)
    proj3 = proj.reshape(nb, seq, IN_COLS)
    a_out = _diff_attention(proj3, meta_proj, band, meta_bias, consts, a_subln.astype(F32), tq)
    b_out = _gqa_attention(bqk.reshape(nb, seq, -1), proj3, meta_bqk, meta_proj, tq, tq)

    h1, h1_b = _out_proj_ln(a_out.reshape(n, A_WIDTH), b_out.reshape(n, B_WIDTH),
                            w_out[0].astype(BF16), x, ln1_g, ln1_b,
                            min(OUT_TM, n), OUT_TK)

    logits = _router(h1, w_router[0], b_router, min(ROUTER_TM, n))
    tm = min(MOE_TM, n)
    n_tiles = (n * TOP_K + N_EXPERTS * (tm - 1)) // tm
    row_tok, row_gate, dest, tile_expert, n_valid = _route(logits, tm, n_tiles)
    x_rows = h1_b[row_tok]
    dff = w2.shape[2]
    act = _expert_ffn1(x_rows, w1[0], b1[0].reshape(N_EXPERTS, 1, -1), tile_expert, n_valid,
                       tm, min(MOE_TN, dff))
    y_rows = _expert_ffn2(act, w2[0], b2[0].reshape(N_EXPERTS, 1, -1), row_gate[:, None],
                          tile_expert, n_valid, tm, min(MOE_TN, d))
    y = jnp.sum(y_rows[dest], axis=1)

    out = _residual_ln(h1, y, ln2_g, ln2_b, min(LN_TM, n))
    n_p = nb_p * seq
    return (out[:n_p].reshape(nb_p, seq, d), out[n_p:].reshape(nb_s, seq, d))
```

```python
import functools
import math

import jax
import jax.numpy as jnp
from jax import lax
from jax.experimental import pallas as pl
from jax.experimental.pallas import tpu as pltpu

HEAD_DIM = 128
N_META = 16
GRID_W = 64
A_HEADS = 8
A_WIDTH = A_HEADS * 2 * HEAD_DIM
B_HEADS = 16
B_KV_HEADS = 4
B_GROUP = B_HEADS // B_KV_HEADS
B_WIDTH = B_HEADS * HEAD_DIM
B_KV_WIDTH = B_KV_HEADS * HEAD_DIM
IN_COLS = 3 * A_WIDTH + B_WIDTH + 2 * B_KV_WIDTH
ROPE_THETA = 10000.0
ROPE_AXIS_DIM = HEAD_DIM // 2
REL_BUCKETS = 32
REL_MAX_DIST = 128
N_EXPERTS = 32
TOP_K = 4
SWIGLU_ALPHA = 1.702
SWIGLU_LIMIT = 7.0
DEPTH = 1
DEEPNORM_ALPHA = (2 * DEPTH) ** 0.25
LN_EPS = 1e-5
RMS_EPS = 1e-6
LAMBDA_INIT = 0.8 - 0.6 * math.exp(-0.3 * 0)
LOG2E = math.log2(math.e)
QUERY_SCALE = HEAD_DIM ** -0.5 * LOG2E

LANES = 128
META_PAD = 128
MASK_VALUE = -1e30
VMEM_LIMIT = 56 * 1024 * 1024

BF16 = jnp.bfloat16
F32 = jnp.float32


def _cparams(sem):
    return pltpu.CompilerParams(dimension_semantics=sem, vmem_limit_bytes=VMEM_LIMIT)


def _proj_kernel(a_ref, b_ref, s_ref, o_ref):
    acc = jnp.dot(a_ref[...], b_ref[...], preferred_element_type=F32)
    o_ref[...] = (acc * s_ref[...]).astype(o_ref.dtype)


def _projection(a, b, colscale, tm, tn):
    m, k = a.shape
    n = b.shape[1]
    tm = min(tm, m)
    return pl.pallas_call(
        _proj_kernel,
        grid=(m // tm, n // tn),
        in_specs=[
            pl.BlockSpec((tm, k), lambda i, j: (i, 0)),
            pl.BlockSpec((k, tn), lambda i, j: (0, j)),
            pl.BlockSpec((1, tn), lambda i, j: (0, j)),
        ],
        out_specs=pl.BlockSpec((tm, tn), lambda i, j: (i, j)),
        out_shape=jax.ShapeDtypeStruct((m, n), BF16),
        compiler_params=_cparams(("parallel", "parallel")),
        name="in_proj",
    )(a, b, colscale)


def _qk_prep_kernel(x_ref, g_ref, c_ref, s_ref, o_ref):
    cos = c_ref[...]
    sin = s_ref[...]
    lane = lax.broadcasted_iota(jnp.int32, cos.shape, 1)
    first_half = (lane % (ROPE_AXIS_DIM)) < (ROPE_AXIS_DIM // 2)
    for hh in range(x_ref.shape[1] // HEAD_DIM):
        sl = slice(hh * HEAD_DIM, (hh + 1) * HEAD_DIM)
        x = x_ref[:, sl].astype(F32)
        ms = jnp.mean(x * x, axis=-1, keepdims=True)
        y = x * lax.rsqrt(ms + RMS_EPS) * g_ref[:, sl]
        partner = jnp.where(first_half,
                            pltpu.roll(y, HEAD_DIM - ROPE_AXIS_DIM // 2, 1),
                            pltpu.roll(y, ROPE_AXIS_DIM // 2, 1))
        o_ref[:, sl] = (y * cos + partner * sin).astype(o_ref.dtype)


def _qk_prep(proj, gains, cos, sin, tm):
    m = proj.shape[0]
    tm = min(tm, m)
    w = B_KV_WIDTH
    first = (3 * A_WIDTH) // w
    nblk = (B_WIDTH + B_KV_WIDTH) // w
    ntab = cos.shape[0] // tm
    return pl.pallas_call(
        _qk_prep_kernel,
        grid=(m // tm, nblk),
        in_specs=[
            pl.BlockSpec((tm, w), lambda i, j: (i, first + j)),
            pl.BlockSpec((None, 1, w), lambda i, j: (j, 0, 0)),
            pl.BlockSpec((tm, HEAD_DIM), lambda i, j: (i % ntab, 0)),
            pl.BlockSpec((tm, HEAD_DIM), lambda i, j: (i % ntab, 0)),
        ],
        out_specs=pl.BlockSpec((tm, w), lambda i, j: (i, j)),
        out_shape=jax.ShapeDtypeStruct((m, B_WIDTH + B_KV_WIDTH), BF16),
        compiler_params=_cparams(("parallel", "parallel")),
        name="qk_prep",
    )(proj, gains, cos, sin)


def _scores_t(k, q):
    return lax.dot_general(k, q, (((1,), (1,)), ((), ())), preferred_element_type=F32)


def _softmax_init(s, v_t, m_ref, l_ref, acc_ref):
    m = jnp.max(s, axis=0, keepdims=True)
    p = jnp.exp2(s - m)
    m_ref[...] = m
    l_ref[...] = jnp.sum(p, axis=0, keepdims=True)
    acc_ref[...] = jnp.dot(v_t, p.astype(v_t.dtype), preferred_element_type=F32)


def _softmax_update(s, shift, v_t, m_ref, l_ref, acc_ref):
    m_prev = m_ref[...]
    m_new = jnp.maximum(m_prev, jnp.max(s, axis=0, keepdims=True) + shift)
    p = jnp.exp2(s - (m_new - shift))
    alpha = jnp.exp2(m_prev - m_new)
    l_ref[...] = alpha * l_ref[...] + jnp.sum(p, axis=0, keepdims=True)
    acc_ref[...] = alpha * acc_ref[...] + jnp.dot(v_t, p.astype(v_t.dtype),
                                                  preferred_element_type=F32)
    m_ref[...] = m_new


def _diff_attn_kernel(c_ref, q1_ref, q2_ref, k1_ref, k2_ref, vt_ref, mk1_ref, mk2_ref, mvt_ref,
                      w_ref, g_ref, o_ref, band_ref, mbias_ref,
                      m1_ref, l1_ref, acc1_ref, m2_ref, l2_ref, acc2_ref, *, tk):
    h = pl.program_id(1)
    i = pl.program_id(2)
    tq = q1_ref.shape[0]
    nk = k1_ref.shape[0] // tk
    bias_left = c_ref[h]
    bias_right = c_ref[A_HEADS + h]
    lam = c_ref[2 * A_HEADS]
    qs = (q1_ref[...], q2_ref[...])
    krefs = (k1_ref, k2_ref)
    stats = ((m1_ref, l1_ref, acc1_ref), (m2_ref, l2_ref, acc2_ref))

    @pl.when(i == 0)
    def _():
        width = w_ref.shape[1]
        rolled = pltpu.roll(jnp.broadcast_to(w_ref[...], (tk, width)), 0, 1,
                            stride=1, stride_axis=0)
        for d in range(3):
            band_ref[d] = rolled[:, (3 - d) * tq:(4 - d) * tq]
        mrolled = pltpu.roll(jnp.broadcast_to(w_ref[...], (META_PAD, width)), width - N_META, 1,
                             stride=1, stride_axis=0)
        mbias_ref[...] = mrolled[:, 2 * tq:3 * tq]

    meta_ok = lax.broadcasted_iota(jnp.int32, mbias_ref.shape, 0) < N_META
    meta_bias = jnp.where(i == 0, mbias_ref[...], bias_left)
    mvt = mvt_ref[...]
    for a, mk_ref in enumerate((mk1_ref, mk2_ref)):
        s = jnp.where(meta_ok, _scores_t(mk_ref[...], qs[a]) + meta_bias, MASK_VALUE)
        _softmax_init(s, mvt, *stats[a])

    def rows(j):
        return pl.ds(pl.multiple_of(j * tk, tk), tk)

    def const_step(shift):
        def body(j, carry):
            vt = vt_ref[j]
            for a in range(2):
                _softmax_update(_scores_t(krefs[a][rows(j), :], qs[a]), shift, vt, *stats[a])
            return carry
        return body

    def band_step(j, d):
        vt = vt_ref[j]
        for a in range(2):
            s = _scores_t(krefs[a][rows(j), :], qs[a]) + band_ref[d]
            _softmax_update(s, 0.0, vt, *stats[a])

    lax.fori_loop(0, jnp.maximum(i - 1, 0), const_step(bias_left), 0)

    @pl.when(i >= 1)
    def _():
        band_step(i - 1, 0)

    band_step(i, 1)

    @pl.when(i + 1 < nk)
    def _():
        band_step(i + 1, 2)

    lax.fori_loop(jnp.minimum(i + 2, nk), nk, const_step(bias_right), 0)

    o = acc1_ref[...] * (1.0 / l1_ref[...]) - lam * (acc2_ref[...] * (1.0 / l2_ref[...]))
    ms = jnp.mean(o * o, axis=0, keepdims=True)
    o = o * lax.rsqrt(ms + RMS_EPS) * g_ref[...] * (1.0 - LAMBDA_INIT)
    o_ref[...] = o.T.astype(o_ref.dtype)


def _diff_attention(proj3, vt, meta_proj, meta_vt, w_bias, consts, subln, tq):
    nb, seq, _ = proj3.shape
    dv = 2 * HEAD_DIM
    nk = seq // tq
    kq2 = A_HEADS
    kk1 = A_WIDTH // HEAD_DIM
    kk2 = kk1 + A_HEADS
    stat = [pltpu.VMEM((1, tq), F32), pltpu.VMEM((1, tq), F32), pltpu.VMEM((dv, tq), F32)]
    return pl.pallas_call(
        functools.partial(_diff_attn_kernel, tk=tq),
        grid=(nb, A_HEADS, nk),
        in_specs=[
            pl.BlockSpec(memory_space=pltpu.SMEM),
            pl.BlockSpec((None, tq, HEAD_DIM), lambda b, h, i: (b, i, h)),
            pl.BlockSpec((None, tq, HEAD_DIM), lambda b, h, i: (b, i, kq2 + h)),
            pl.BlockSpec((None, seq, HEAD_DIM), lambda b, h, i: (b, 0, kk1 + h)),
            pl.BlockSpec((None, seq, HEAD_DIM), lambda b, h, i: (b, 0, kk2 + h)),
            pl.BlockSpec((None, None, nk, dv, tq), lambda b, h, i: (b, h, 0, 0, 0)),
            pl.BlockSpec((META_PAD, HEAD_DIM), lambda b, h, i: (0, kk1 + h)),
            pl.BlockSpec((META_PAD, HEAD_DIM), lambda b, h, i: (0, kk2 + h)),
            pl.BlockSpec((None, dv, META_PAD), lambda b, h, i: (h, 0, 0)),
            pl.BlockSpec((None, 1, 4 * tq), lambda b, h, i: (h, 0, 0)),
            pl.BlockSpec((dv, 1), lambda b, h, i: (0, 0)),
        ],
        out_specs=pl.BlockSpec((None, tq, dv), lambda b, h, i: (b, i, h)),
        out_shape=jax.ShapeDtypeStruct((nb, seq, A_WIDTH), BF16),
        scratch_shapes=[pltpu.VMEM((3, tq, tq), F32), pltpu.VMEM((META_PAD, tq), F32)] + stat + stat,
        compiler_params=_cparams(("parallel", "parallel", "arbitrary")),
        name="diff_attn",
    )(consts, proj3, proj3, proj3, proj3, vt, meta_proj, meta_proj, meta_vt, w_bias, subln)


def _gqa_kernel(q_ref, k_ref, vt_ref, mk_ref, mvt_ref, o_ref, *stat_refs, tk, heads):
    nk = k_ref.shape[0] // tk
    tq = q_ref.shape[0]
    qs = [q_ref[:, g * HEAD_DIM:(g + 1) * HEAD_DIM] for g in range(heads)]
    stats = [stat_refs[3 * g:3 * g + 3] for g in range(heads)]
    meta_ok = lax.broadcasted_iota(jnp.int32, (META_PAD, tq), 0) < N_META
    mk = mk_ref[...]
    mvt = mvt_ref[...]
    for g in range(heads):
        s = jnp.where(meta_ok, _scores_t(mk, qs[g]), MASK_VALUE)
        _softmax_init(s, mvt, *stats[g])

    def body(j, carry):
        k = k_ref[pl.ds(pl.multiple_of(j * tk, tk), tk), :]
        vt = vt_ref[j]
        for g in range(heads):
            _softmax_update(_scores_t(k, qs[g]), 0.0, vt, *stats[g])
        return carry

    lax.fori_loop(0, nk, body, 0)
    for g in range(heads):
        m_ref, l_ref, acc_ref = stats[g]
        o = acc_ref[...] * (1.0 / l_ref[...])
        o_ref[:, g * HEAD_DIM:(g + 1) * HEAD_DIM] = o.T.astype(o_ref.dtype)


def _gqa_attention(bqk3, vt, meta_bqk, meta_vt, tq, heads):
    nb, seq, _ = bqk3.shape
    nk = seq // tq
    kk = B_HEADS
    per_group = B_GROUP // heads
    stat = [pltpu.VMEM((1, tq), F32), pltpu.VMEM((1, tq), F32), pltpu.VMEM((HEAD_DIM, tq), F32)]
    return pl.pallas_call(
        functools.partial(_gqa_kernel, tk=tq, heads=heads),
        grid=(nb, B_HEADS // heads, nk),
        in_specs=[
            pl.BlockSpec((None, tq, heads * HEAD_DIM), lambda b, h, i: (b, i, h)),
            pl.BlockSpec((None, seq, HEAD_DIM), lambda b, h, i: (b, 0, kk + h // per_group)),
            pl.BlockSpec((None, None, nk, HEAD_DIM, tq),
                         lambda b, h, i: (b, h // per_group, 0, 0, 0)),
            pl.BlockSpec((META_PAD, HEAD_DIM), lambda b, h, i: (0, kk + h // per_group)),
            pl.BlockSpec((None, HEAD_DIM, META_PAD), lambda b, h, i: (h // per_group, 0, 0)),
        ],
        out_specs=pl.BlockSpec((None, tq, heads * HEAD_DIM), lambda b, h, i: (b, i, h)),
        out_shape=jax.ShapeDtypeStruct((nb, seq, B_WIDTH), BF16),
        scratch_shapes=stat * heads,
        compiler_params=_cparams(("parallel", "parallel", "parallel")),
        name="gqa_attn",
    )(bqk3, bqk3, vt, meta_bqk, meta_vt)


def _chunked_transpose(v3, meta_v, heads, dv, tk):
    nb, seq, _ = v3.shape
    vt = v3.reshape(nb, seq // tk, tk, heads, dv).transpose(0, 3, 1, 4, 2)
    meta_vt = meta_v.reshape(META_PAD, heads, dv).transpose(1, 2, 0)
    return vt, meta_vt


def _layer_norm(y, g, b):
    mu = jnp.mean(y, axis=-1, keepdims=True)
    yc = y - mu
    var = jnp.mean(yc * yc, axis=-1, keepdims=True)
    return yc * lax.rsqrt(var + LN_EPS) * g + b


def _out_proj_ln_kernel(a_ref, b_ref, w_ref, x_ref, g_ref, beta_ref, h_ref, hb_ref, acc_ref, *, ka):
    k = pl.program_id(1)

    @pl.when(k == 0)
    def _():
        acc_ref[...] = jnp.zeros_like(acc_ref)

    @pl.when(k < ka)
    def _():
        acc_ref[...] += jnp.dot(a_ref[...], w_ref[...], preferred_element_type=F32)

    @pl.when(k >= ka)
    def _():
        acc_ref[...] += jnp.dot(b_ref[...], w_ref[...], preferred_element_type=F32)

    @pl.when(k == pl.num_programs(1) - 1)
    def _():
        h = _layer_norm(DEEPNORM_ALPHA * x_ref[...] + acc_ref[...], g_ref[...], beta_ref[...])
        h_ref[...] = h
        hb_ref[...] = h.astype(hb_ref.dtype)


def _out_proj_ln(a_out, b_out, w_out, x, g, beta, tm, tk):
    m, d = x.shape
    ka = a_out.shape[1] // tk
    kb = b_out.shape[1] // tk
    return pl.pallas_call(
        functools.partial(_out_proj_ln_kernel, ka=ka),
        grid=(m // tm, ka + kb),
        in_specs=[
            pl.BlockSpec((tm, tk), lambda i, k: (i, jnp.minimum(k, ka - 1))),
            pl.BlockSpec((tm, tk), lambda i, k: (i, jnp.maximum(k - ka, 0))),
            pl.BlockSpec((tk, d), lambda i, k: (k, 0)),
            pl.BlockSpec((tm, d), lambda i, k: (i, 0)),
            pl.BlockSpec((1, d), lambda i, k: (0, 0)),
            pl.BlockSpec((1, d), lambda i, k: (0, 0)),
        ],
        out_specs=[
            pl.BlockSpec((tm, d), lambda i, k: (i, 0)),
            pl.BlockSpec((tm, d), lambda i, k: (i, 0)),
        ],
        out_shape=[jax.ShapeDtypeStruct((m, d), F32), jax.ShapeDtypeStruct((m, d), BF16)],
        scratch_shapes=[pltpu.VMEM((tm, d), F32)],
        compiler_params=_cparams(("parallel", "arbitrary")),
        name="out_proj_ln1",
    )(a_out, b_out, w_out, x, g, beta)


def _router_kernel(h_ref, w_ref, b_ref, o_ref):
    o_ref[...] = jnp.dot(h_ref[...], w_ref[...], precision=lax.Precision.HIGHEST,
                         preferred_element_type=F32) + b_ref[...]


def _router(h, w, b, tm):
    m, d = h.shape
    e = w.shape[1]
    return pl.pallas_call(
        _router_kernel,
        grid=(m // tm,),
        in_specs=[
            pl.BlockSpec((tm, d), lambda i: (i, 0)),
            pl.BlockSpec((d, e), lambda i: (0, 0)),
            pl.BlockSpec((1, e), lambda i: (0, 0)),
        ],
        out_specs=pl.BlockSpec((tm, e), lambda i: (i, 0)),
        out_shape=jax.ShapeDtypeStruct((m, e), F32),
        compiler_params=_cparams(("parallel",)),
        name="router",
    )(h, w, b)


def _ffn1_kernel(te_ref, nv_ref, x_ref, wg_ref, wl_ref, bg_ref, bl_ref, o_ref):
    @pl.when(pl.program_id(0) < nv_ref[0])
    def _():
        x = x_ref[...]
        hg = jnp.dot(x, wg_ref[...].astype(BF16), preferred_element_type=F32) + bg_ref[...]
        hl = jnp.dot(x, wl_ref[...].astype(BF16), preferred_element_type=F32) + bl_ref[...]
        hg = jnp.minimum(hg, SWIGLU_LIMIT)
        hl = jnp.clip(hl, -SWIGLU_LIMIT, SWIGLU_LIMIT)
        act = hg * (1.0 / (1.0 + jnp.exp(-SWIGLU_ALPHA * hg))) * (hl + 1.0)
        o_ref[...] = act.astype(o_ref.dtype)


def _ffn2_kernel(te_ref, nv_ref, a_ref, w_ref, b_ref, g_ref, o_ref):
    @pl.when(pl.program_id(0) < nv_ref[0])
    def _():
        y = jnp.dot(a_ref[...], w_ref[...].astype(BF16), preferred_element_type=F32) + b_ref[...]
        o_ref[...] = (y * g_ref[...]).astype(o_ref.dtype)


def _grouped_maps(ncol):
    def row(t, j, te, nv):
        return jnp.minimum(t, nv[0] - 1)

    def col(t, j, te, nv):
        return jnp.where(t < nv[0], j, ncol - 1)

    def expert(t, j, te, nv):
        return te[row(t, j, te, nv)]

    return row, col, expert


def _expert_ffn1(x_rows, w1, b1, tile_expert, n_valid, tm, tn):
    n_rows, d = x_rows.shape
    dff = w1.shape[2] // 2
    ncol = dff // tn
    row, col, expert = _grouped_maps(ncol)
    grid_spec = pltpu.PrefetchScalarGridSpec(
        num_scalar_prefetch=2,
        grid=(n_rows // tm, ncol),
        in_specs=[
            pl.BlockSpec((tm, d), lambda *a: (row(*a), 0)),
            pl.BlockSpec((None, d, tn), lambda *a: (expert(*a), 0, col(*a))),
            pl.BlockSpec((None, d, tn), lambda *a: (expert(*a), 0, ncol + col(*a))),
            pl.BlockSpec((None, 1, tn), lambda *a: (expert(*a), 0, col(*a))),
            pl.BlockSpec((None, 1, tn), lambda *a: (expert(*a), 0, ncol + col(*a))),
        ],
        out_specs=pl.BlockSpec((tm, tn), lambda *a: (row(*a), col(*a))),
    )
    return pl.pallas_call(
        _ffn1_kernel,
        grid_spec=grid_spec,
        out_shape=jax.ShapeDtypeStruct((n_rows, dff), BF16),
        compiler_params=_cparams(("arbitrary", "arbitrary")),
        name="moe_ffn1",
    )(tile_expert, n_valid, x_rows, w1, w1, b1, b1)


def _expert_ffn2(act, w2, b2, row_gate, tile_expert, n_valid, tm, tn):
    n_rows, dff = act.shape
    d = w2.shape[2]
    ncol = d // tn
    row, col, expert = _grouped_maps(ncol)
    grid_spec = pltpu.PrefetchScalarGridSpec(
        num_scalar_prefetch=2,
        grid=(n_rows // tm, ncol),
        in_specs=[
            pl.BlockSpec((tm, dff), lambda *a: (row(*a), 0)),
            pl.BlockSpec((None, dff, tn), lambda *a: (expert(*a), 0, col(*a))),
            pl.BlockSpec((None, 1, tn), lambda *a: (expert(*a), 0, col(*a))),
            pl.BlockSpec((tm, 1), lambda *a: (row(*a), 0)),
        ],
        out_specs=pl.BlockSpec((tm, tn), lambda *a: (row(*a), col(*a))),
    )
    return pl.pallas_call(
        _ffn2_kernel,
        grid_spec=grid_spec,
        out_shape=jax.ShapeDtypeStruct((n_rows, d), BF16),
        compiler_params=_cparams(("arbitrary", "arbitrary")),
        name="moe_ffn2",
    )(tile_expert, n_valid, act, w2, b2, row_gate)


def _residual_ln_kernel(h_ref, *refs):
    y_refs, (g_ref, b_ref, o_ref) = refs[:TOP_K], refs[TOP_K:]
    y = y_refs[0][...].astype(F32)
    for y_ref in y_refs[1:]:
        y = y + y_ref[...].astype(F32)
    o_ref[...] = _layer_norm(DEEPNORM_ALPHA * h_ref[...] + y, g_ref[...], b_ref[...])


def _residual_ln(h, ys, g, b, tm):
    m, d = h.shape
    tile = pl.BlockSpec((tm, d), lambda i: (i, 0))
    vec = pl.BlockSpec((1, d), lambda i: (0, 0))
    return pl.pallas_call(
        _residual_ln_kernel,
        grid=(m // tm,),
        in_specs=[tile] * (1 + TOP_K) + [vec, vec],
        out_specs=tile,
        out_shape=jax.ShapeDtypeStruct((m, d), F32),
        compiler_params=_cparams(("parallel",)),
        name="residual_ln2",
    )(h, *ys, g, b)


def _rel_bucket(rel):
    nb = REL_BUCKETS // 2
    max_exact = nb // 2
    ret = jnp.where(rel > 0, nb, 0)
    n = jnp.abs(rel)
    nf = jnp.maximum(n, 1).astype(F32)
    large = max_exact + (jnp.log(nf / max_exact) / math.log(REL_MAX_DIST / max_exact)
                         * (nb - max_exact)).astype(jnp.int32)
    large = jnp.minimum(large, nb - 1)
    return ret + jnp.where(n < max_exact, n, large)


def _bias_tables(rel_bias, tq):
    assert tq >= REL_MAX_DIST
    rel = 2 * tq - jnp.arange(4 * tq, dtype=jnp.int32)
    w = rel_bias[_rel_bucket(rel)].astype(F32).T[:, None, :]
    far = rel_bias[_rel_bucket(jnp.array([-(tq + 1), tq + 1], jnp.int32))].astype(F32)
    return w, far


def _rope_tables(seq):
    tok = jnp.arange(seq, dtype=jnp.int32)
    row_id = (tok // GRID_W).astype(F32)
    col_id = (tok % GRID_W).astype(F32)
    inv = ROPE_THETA ** (-jnp.arange(0, ROPE_AXIS_DIM, 2, dtype=F32) / ROPE_AXIS_DIM)
    ang_r = row_id[:, None] * inv[None, :]
    ang_c = col_id[:, None] * inv[None, :]
    cos = jnp.concatenate([jnp.cos(ang_r)] * 2 + [jnp.cos(ang_c)] * 2, axis=-1)
    sin = jnp.concatenate([-jnp.sin(ang_r), jnp.sin(ang_r), -jnp.sin(ang_c), jnp.sin(ang_c)], axis=-1)
    return cos, sin


def _route(logits, tm, n_tiles):
    n = logits.shape[0]
    top_val, top_idx = lax.top_k(logits, TOP_K)
    gate = jax.nn.softmax(top_val, axis=-1)
    n_assign = n * TOP_K
    e_flat = top_idx.reshape(-1)
    g_flat = gate.reshape(-1)
    counts = jnp.zeros((N_EXPERTS,), jnp.int32).at[e_flat].add(1)
    tiles_per = (counts + tm - 1) // tm
    tile_end = jnp.cumsum(tiles_per)
    pstart = (tile_end - tiles_per) * tm
    start = jnp.cumsum(counts) - counts
    order = jnp.argsort(e_flat)
    e_sorted = e_flat[order]
    dest_sorted = pstart[e_sorted] + (jnp.arange(n_assign, dtype=jnp.int32) - start[e_sorted])
    n_rows = n_tiles * tm
    row_tok = jnp.zeros((n_rows,), jnp.int32).at[dest_sorted].set((order // TOP_K).astype(jnp.int32))
    row_gate = jnp.zeros((n_rows,), F32).at[dest_sorted].set(g_flat[order])
    dest = jnp.zeros((n_assign,), jnp.int32).at[order].set(dest_sorted).reshape(n, TOP_K)
    tile_expert = jnp.minimum(
        jnp.searchsorted(tile_end, jnp.arange(n_tiles, dtype=jnp.int32), side='right'),
        N_EXPERTS - 1).astype(jnp.int32)
    n_valid = tile_end[-1:].astype(jnp.int32)
    return row_tok, row_gate, dest, tile_expert, n_valid


PROJ_TM, PROJ_TN = 1024, 1024
PREP_TM = 512
ATTN_TQ = 512
GQA_HEADS_PER_STEP = 4
OUT_TM, OUT_TK = 256, 512
ROUTER_TM = 512
MOE_TM, MOE_TN = 1024, 256
LN_TM = 256


def kernel(x_prompt, x_sample, meta_tokens, rel_bias, w_in, lambda_q1, lambda_k1, lambda_q2,
           lambda_k2, a_subln, q_norm, k_norm, w_out, ln1_g, ln1_b, w_router, b_router,
           w1, b1, w2, b2, ln2_g, ln2_b):
    d = x_prompt.shape[-1]
    seq = x_prompt.shape[1]
    assert x_sample.shape[1] == seq and seq % GRID_W == 0
    nb_p, nb_s = x_prompt.shape[0], x_sample.shape[0]
    nb = nb_p + nb_s
    x = jnp.concatenate([x_prompt.reshape(-1, d), x_sample.reshape(-1, d)], axis=0)
    n = x.shape[0]
    tq = min(ATTN_TQ, seq)

    w_in_b = w_in[0].astype(BF16)
    colscale = jnp.concatenate([jnp.full((A_WIDTH,), QUERY_SCALE, F32),
                                jnp.ones((IN_COLS - A_WIDTH,), F32)])[None]
    tn = min(PROJ_TN, IN_COLS)
    proj = _projection(x.astype(BF16), w_in_b, colscale, PROJ_TM, tn)
    meta_rows = jnp.zeros((META_PAD, d), F32).at[:N_META].set(meta_tokens)
    meta_proj = _projection(meta_rows.astype(BF16), w_in_b, colscale, META_PAD, tn)

    gains = jnp.concatenate([jnp.tile(q_norm[0] * QUERY_SCALE, B_HEADS),
                             jnp.tile(k_norm[0], B_KV_HEADS)]).reshape(-1, 1, B_KV_WIDTH)
    cos, sin = _rope_tables(seq)
    bqk = _qk_prep(proj, gains, cos, sin, min(PREP_TM, seq))
    meta_bqk = _qk_prep(meta_proj, gains, jnp.ones((META_PAD, HEAD_DIM), F32),
                        jnp.zeros((META_PAD, HEAD_DIM), F32), META_PAD)

    lam = (jnp.exp(jnp.sum(lambda_q1[0].astype(F32) * lambda_k1[0].astype(F32)))
           - jnp.exp(jnp.sum(lambda_q2[0].astype(F32) * lambda_k2[0].astype(F32))) + LAMBDA_INIT)
    w_bias, far = _bias_tables(rel_bias * LOG2E, tq)
    consts = jnp.concatenate([far[0], far[1], lam[None]]).astype(F32)
    proj3 = proj.reshape(nb, seq, IN_COLS)
    a_v0, b_v0 = 2 * A_WIDTH, 3 * A_WIDTH + B_WIDTH + B_KV_WIDTH
    vt_a, meta_vt_a = _chunked_transpose(proj3[:, :, a_v0:a_v0 + A_WIDTH],
                                         meta_proj[:, a_v0:a_v0 + A_WIDTH], A_HEADS, 2 * HEAD_DIM, tq)
    vt_b, meta_vt_b = _chunked_transpose(proj3[:, :, b_v0:], meta_proj[:, b_v0:],
                                         B_KV_HEADS, HEAD_DIM, tq)
    a_out = _diff_attention(proj3, vt_a, meta_proj, meta_vt_a, w_bias, consts,
                            a_subln.astype(F32).reshape(-1, 1), tq)
    b_out = _gqa_attention(bqk.reshape(nb, seq, -1), vt_b, meta_bqk, meta_vt_b, tq,
                           GQA_HEADS_PER_STEP)

    h1, h1_b = _out_proj_ln(a_out.reshape(n, A_WIDTH), b_out.reshape(n, B_WIDTH),
                            w_out[0].astype(BF16), x, ln1_g, ln1_b,
                            min(OUT_TM, n), OUT_TK)

    logits = _router(h1, w_router[0], b_router, min(ROUTER_TM, n))
    tm = min(MOE_TM, n)
    n_tiles = (n * TOP_K + N_EXPERTS * (tm - 1)) // tm
    row_tok, row_gate, dest, tile_expert, n_valid = _route(logits, tm, n_tiles)
    x_rows = h1_b[row_tok]
    dff = w2.shape[2]
    act = _expert_ffn1(x_rows, w1[0], b1[0].reshape(N_EXPERTS, 1, -1), tile_expert, n_valid,
                       tm, min(MOE_TN, dff))
    y_rows = _expert_ffn2(act, w2[0], b2[0].reshape(N_EXPERTS, 1, -1), row_gate[:, None],
                          tile_expert, n_valid, tm, min(MOE_TN, d))
    ys = [y_rows[dest[:, k]] for k in range(TOP_K)]

    out = _residual_ln(h1, ys, ln2_g, ln2_b, min(LN_TM, n))
    n_p = nb_p * seq
    return (out[:n_p].reshape(nb_p, seq, d), out[n_p:].reshape(nb_s, seq, d))
```

```python
import functools
import math

import jax
import jax.numpy as jnp
from jax import lax
from jax.experimental import pallas as pl
from jax.experimental.pallas import tpu as pltpu

HEAD_DIM = 128
N_META = 16
GRID_W = 64
A_HEADS = 8
A_WIDTH = A_HEADS * 2 * HEAD_DIM
B_HEADS = 16
B_KV_HEADS = 4
B_GROUP = B_HEADS // B_KV_HEADS
B_WIDTH = B_HEADS * HEAD_DIM
B_KV_WIDTH = B_KV_HEADS * HEAD_DIM
IN_COLS = 3 * A_WIDTH + B_WIDTH + 2 * B_KV_WIDTH
ROPE_THETA = 10000.0
ROPE_AXIS_DIM = HEAD_DIM // 2
REL_BUCKETS = 32
REL_MAX_DIST = 128
N_EXPERTS = 32
TOP_K = 4
SWIGLU_ALPHA = 1.702
SWIGLU_LIMIT = 7.0
DEPTH = 1
DEEPNORM_ALPHA = (2 * DEPTH) ** 0.25
LN_EPS = 1e-5
RMS_EPS = 1e-6
LAMBDA_INIT = 0.8 - 0.6 * math.exp(-0.3 * 0)
LOG2E = math.log2(math.e)
QUERY_SCALE = HEAD_DIM ** -0.5 * LOG2E

LANES = 128
META_PAD = 128
MASK_VALUE = -1e30
VMEM_LIMIT = 56 * 1024 * 1024

BF16 = jnp.bfloat16
F32 = jnp.float32


def _cparams(sem):
    return pltpu.CompilerParams(dimension_semantics=sem, vmem_limit_bytes=VMEM_LIMIT)


def _proj_kernel(a_ref, b_ref, s_ref, o_ref):
    acc = jnp.dot(a_ref[...], b_ref[...], preferred_element_type=F32)
    o_ref[...] = (acc * s_ref[...]).astype(o_ref.dtype)


def _projection(a, b, colscale, tm, tn):
    m, k = a.shape
    n = b.shape[1]
    tm = min(tm, m)
    return pl.pallas_call(
        _proj_kernel,
        grid=(m // tm, n // tn),
        in_specs=[
            pl.BlockSpec((tm, k), lambda i, j: (i, 0)),
            pl.BlockSpec((k, tn), lambda i, j: (0, j)),
            pl.BlockSpec((1, tn), lambda i, j: (0, j)),
        ],
        out_specs=pl.BlockSpec((tm, tn), lambda i, j: (i, j)),
        out_shape=jax.ShapeDtypeStruct((m, n), BF16),
        compiler_params=_cparams(("parallel", "parallel")),
        name="in_proj",
    )(a, b, colscale)


def _qk_prep_kernel(x_ref, g_ref, c_ref, s_ref, o_ref):
    cos = c_ref[...]
    sin = s_ref[...]
    lane = lax.broadcasted_iota(jnp.int32, cos.shape, 1)
    first_half = (lane % (ROPE_AXIS_DIM)) < (ROPE_AXIS_DIM // 2)
    for hh in range(x_ref.shape[1] // HEAD_DIM):
        sl = slice(hh * HEAD_DIM, (hh + 1) * HEAD_DIM)
        x = x_ref[:, sl].astype(F32)
        ms = jnp.mean(x * x, axis=-1, keepdims=True)
        y = x * lax.rsqrt(ms + RMS_EPS) * g_ref[:, sl]
        partner = jnp.where(first_half,
                            pltpu.roll(y, HEAD_DIM - ROPE_AXIS_DIM // 2, 1),
                            pltpu.roll(y, ROPE_AXIS_DIM // 2, 1))
        o_ref[:, sl] = (y * cos + partner * sin).astype(o_ref.dtype)


def _qk_prep(proj, gains, cos, sin, tm):
    m = proj.shape[0]
    tm = min(tm, m)
    w = B_KV_WIDTH
    first = (3 * A_WIDTH) // w
    nblk = (B_WIDTH + B_KV_WIDTH) // w
    ntab = cos.shape[0] // tm
    return pl.pallas_call(
        _qk_prep_kernel,
        grid=(m // tm, nblk),
        in_specs=[
            pl.BlockSpec((tm, w), lambda i, j: (i, first + j)),
            pl.BlockSpec((None, 1, w), lambda i, j: (j, 0, 0)),
            pl.BlockSpec((tm, HEAD_DIM), lambda i, j: (i % ntab, 0)),
            pl.BlockSpec((tm, HEAD_DIM), lambda i, j: (i % ntab, 0)),
        ],
        out_specs=pl.BlockSpec((tm, w), lambda i, j: (i, j)),
        out_shape=jax.ShapeDtypeStruct((m, B_WIDTH + B_KV_WIDTH), BF16),
        compiler_params=_cparams(("parallel", "parallel")),
        name="qk_prep",
    )(proj, gains, cos, sin)


def _scores_t(k, q):
    return lax.dot_general(k, q, (((1,), (1,)), ((), ())), preferred_element_type=F32)


def _softmax_init(s, v_t, m_ref, l_ref, acc_ref):
    m = jnp.max(s, axis=0, keepdims=True)
    p = jnp.exp2(s - m)
    m_ref[...] = m
    l_ref[...] = jnp.sum(p, axis=0, keepdims=True)
    acc_ref[...] = jnp.dot(v_t, p.astype(v_t.dtype), preferred_element_type=F32)


def _softmax_update(s, shift, v_t, m_ref, l_ref, acc_ref):
    m_prev = m_ref[...]
    m_new = jnp.maximum(m_prev, jnp.max(s, axis=0, keepdims=True) + shift)
    p = jnp.exp2(s - (m_new - shift))
    alpha = jnp.exp2(m_prev - m_new)
    l_ref[...] = alpha * l_ref[...] + jnp.sum(p, axis=0, keepdims=True)
    acc_ref[...] = alpha * acc_ref[...] + jnp.dot(v_t, p.astype(v_t.dtype),
                                                  preferred_element_type=F32)
    m_ref[...] = m_new


def _diff_attn_kernel(c_ref, q1_ref, q2_ref, k1_ref, k2_ref, vt_ref, mk1_ref, mk2_ref, mvt_ref,
                      w_ref, g_ref, o_ref, band_ref, mbias_ref,
                      m1_ref, l1_ref, acc1_ref, m2_ref, l2_ref, acc2_ref, *, tk):
    h = pl.program_id(1)
    i = pl.program_id(2)
    tq = q1_ref.shape[0]
    nk = k1_ref.shape[0] // tk
    bias_left = c_ref[h]
    bias_right = c_ref[A_HEADS + h]
    lam = c_ref[2 * A_HEADS]
    qs = (q1_ref[...], q2_ref[...])
    krefs = (k1_ref, k2_ref)
    stats = ((m1_ref, l1_ref, acc1_ref), (m2_ref, l2_ref, acc2_ref))

    @pl.when(i == 0)
    def _():
        width = w_ref.shape[1]
        rolled = pltpu.roll(jnp.broadcast_to(w_ref[...], (tk, width)), 0, 1,
                            stride=1, stride_axis=0)
        for d in range(3):
            band_ref[d] = rolled[:, (3 - d) * tq:(4 - d) * tq]
        mrolled = pltpu.roll(jnp.broadcast_to(w_ref[...], (META_PAD, width)), width - N_META, 1,
                             stride=1, stride_axis=0)
        mbias_ref[...] = mrolled[:, 2 * tq:3 * tq]

    meta_ok = lax.broadcasted_iota(jnp.int32, mbias_ref.shape, 0) < N_META
    meta_bias = jnp.where(i == 0, mbias_ref[...], bias_left)
    mvt = mvt_ref[...]
    for a, mk_ref in enumerate((mk1_ref, mk2_ref)):
        s = jnp.where(meta_ok, _scores_t(mk_ref[...], qs[a]) + meta_bias, MASK_VALUE)
        _softmax_init(s, mvt, *stats[a])

    def rows(j):
        return pl.ds(pl.multiple_of(j * tk, tk), tk)

    def const_step(shift):
        def body(j, carry):
            vt = vt_ref[j]
            for a in range(2):
                _softmax_update(_scores_t(krefs[a][rows(j), :], qs[a]), shift, vt, *stats[a])
            return carry
        return body

    def band_step(j, d):
        vt = vt_ref[j]
        for a in range(2):
            s = _scores_t(krefs[a][rows(j), :], qs[a]) + band_ref[d]
            _softmax_update(s, 0.0, vt, *stats[a])

    lax.fori_loop(0, jnp.maximum(i - 1, 0), const_step(bias_left), 0)

    @pl.when(i >= 1)
    def _():
        band_step(i - 1, 0)

    band_step(i, 1)

    @pl.when(i + 1 < nk)
    def _():
        band_step(i + 1, 2)

    lax.fori_loop(jnp.minimum(i + 2, nk), nk, const_step(bias_right), 0)

    o = acc1_ref[...] * (1.0 / l1_ref[...]) - lam * (acc2_ref[...] * (1.0 / l2_ref[...]))
    ms = jnp.mean(o * o, axis=0, keepdims=True)
    o = o * lax.rsqrt(ms + RMS_EPS) * g_ref[...] * (1.0 - LAMBDA_INIT)
    o_ref[...] = o.T.astype(o_ref.dtype)


def _diff_attention(proj3, vt, meta_proj, meta_vt, w_bias, consts, subln, tq):
    nb, seq, _ = proj3.shape
    dv = 2 * HEAD_DIM
    nk = seq // tq
    kq2 = A_HEADS
    kk1 = A_WIDTH // HEAD_DIM
    kk2 = kk1 + A_HEADS
    stat = [pltpu.VMEM((1, tq), F32), pltpu.VMEM((1, tq), F32), pltpu.VMEM((dv, tq), F32)]
    return pl.pallas_call(
        functools.partial(_diff_attn_kernel, tk=tq),
        grid=(nb, A_HEADS, nk),
        in_specs=[
            pl.BlockSpec(memory_space=pltpu.SMEM),
            pl.BlockSpec((None, tq, HEAD_DIM), lambda b, h, i: (b, i, h)),
            pl.BlockSpec((None, tq, HEAD_DIM), lambda b, h, i: (b, i, kq2 + h)),
            pl.BlockSpec((None, seq, HEAD_DIM), lambda b, h, i: (b, 0, kk1 + h)),
            pl.BlockSpec((None, seq, HEAD_DIM), lambda b, h, i: (b, 0, kk2 + h)),
            pl.BlockSpec((None, None, nk, dv, tq), lambda b, h, i: (b, h, 0, 0, 0)),
            pl.BlockSpec((META_PAD, HEAD_DIM), lambda b, h, i: (0, kk1 + h)),
            pl.BlockSpec((META_PAD, HEAD_DIM), lambda b, h, i: (0, kk2 + h)),
            pl.BlockSpec((None, dv, META_PAD), lambda b, h, i: (h, 0, 0)),
            pl.BlockSpec((None, 1, 4 * tq), lambda b, h, i: (h, 0, 0)),
            pl.BlockSpec((dv, 1), lambda b, h, i: (0, 0)),
        ],
        out_specs=pl.BlockSpec((None, tq, dv), lambda b, h, i: (b, i, h)),
        out_shape=jax.ShapeDtypeStruct((nb, seq, A_WIDTH), BF16),
        scratch_shapes=[pltpu.VMEM((3, tq, tq), F32), pltpu.VMEM((META_PAD, tq), F32)] + stat + stat,
        compiler_params=_cparams(("parallel", "parallel", "arbitrary")),
        name="diff_attn",
    )(consts, proj3, proj3, proj3, proj3, vt, meta_proj, meta_proj, meta_vt, w_bias, subln)


def _gqa_kernel(q_ref, k_ref, vt_ref, mk_ref, mvt_ref, o_ref, *stat_refs, tk, heads):
    nk = k_ref.shape[0] // tk
    tq = q_ref.shape[0]
    qs = [q_ref[:, g * HEAD_DIM:(g + 1) * HEAD_DIM] for g in range(heads)]
    stats = [stat_refs[3 * g:3 * g + 3] for g in range(heads)]
    meta_ok = lax.broadcasted_iota(jnp.int32, (META_PAD, tq), 0) < N_META
    mk = mk_ref[...]
    mvt = mvt_ref[...]
    for g in range(heads):
        s = jnp.where(meta_ok, _scores_t(mk, qs[g]), MASK_VALUE)
        _softmax_init(s, mvt, *stats[g])

    def body(j, carry):
        k = k_ref[pl.ds(pl.multiple_of(j * tk, tk), tk), :]
        vt = vt_ref[j]
        for g in range(heads):
            _softmax_update(_scores_t(k, qs[g]), 0.0, vt, *stats[g])
        return carry

    lax.fori_loop(0, nk, body, 0)
    for g in range(heads):
        m_ref, l_ref, acc_ref = stats[g]
        o = acc_ref[...] * (1.0 / l_ref[...])
        o_ref[:, g * HEAD_DIM:(g + 1) * HEAD_DIM] = o.T.astype(o_ref.dtype)


def _gqa_attention(bqk3, vt, meta_bqk, meta_vt, tq, heads):
    nb, seq, _ = bqk3.shape
    nk = seq // tq
    kk = B_HEADS
    per_group = B_GROUP // heads
    stat = [pltpu.VMEM((1, tq), F32), pltpu.VMEM((1, tq), F32), pltpu.VMEM((HEAD_DIM, tq), F32)]
    return pl.pallas_call(
        functools.partial(_gqa_kernel, tk=tq, heads=heads),
        grid=(nb, B_HEADS // heads, nk),
        in_specs=[
            pl.BlockSpec((None, tq, heads * HEAD_DIM), lambda b, h, i: (b, i, h)),
            pl.BlockSpec((None, seq, HEAD_DIM), lambda b, h, i: (b, 0, kk + h // per_group)),
            pl.BlockSpec((None, None, nk, HEAD_DIM, tq),
                         lambda b, h, i: (b, h // per_group, 0, 0, 0)),
            pl.BlockSpec((META_PAD, HEAD_DIM), lambda b, h, i: (0, kk + h // per_group)),
            pl.BlockSpec((None, HEAD_DIM, META_PAD), lambda b, h, i: (h // per_group, 0, 0)),
        ],
        out_specs=pl.BlockSpec((None, tq, heads * HEAD_DIM), lambda b, h, i: (b, i, h)),
        out_shape=jax.ShapeDtypeStruct((nb, seq, B_WIDTH), BF16),
        scratch_shapes=stat * heads,
        compiler_params=_cparams(("parallel", "parallel", "parallel")),
        name="gqa_attn",
    )(bqk3, bqk3, vt, meta_bqk, meta_vt)


def _chunked_transpose(v3, meta_v, heads, dv, tk):
    nb, seq, _ = v3.shape
    vt = v3.reshape(nb, seq // tk, tk, heads, dv).transpose(0, 3, 1, 4, 2)
    meta_vt = meta_v.reshape(META_PAD, heads, dv).transpose(1, 2, 0)
    return vt, meta_vt


def _layer_norm(y, g, b):
    mu = jnp.mean(y, axis=-1, keepdims=True)
    yc = y - mu
    var = jnp.mean(yc * yc, axis=-1, keepdims=True)
    return yc * lax.rsqrt(var + LN_EPS) * g + b


def _out_proj_ln_kernel(a_ref, b_ref, w_ref, x_ref, g_ref, beta_ref, *refs, ka):
    h_ref, hb_ref, acc_ref = refs[-3:]
    k = pl.program_id(1)

    @pl.when(k == 0)
    def _():
        acc_ref[...] = jnp.zeros_like(acc_ref)

    @pl.when(k < ka)
    def _():
        acc_ref[...] += jnp.dot(a_ref[...], w_ref[...], preferred_element_type=F32)

    @pl.when(k >= ka)
    def _():
        acc_ref[...] += jnp.dot(b_ref[...], w_ref[...], preferred_element_type=F32)

    @pl.when(k == pl.num_programs(1) - 1)
    def _():
        def body(c, carry):
            sl = pl.ds(pl.multiple_of(c * LN_SLAB, LN_SLAB), LN_SLAB)
            h = _layer_norm(DEEPNORM_ALPHA * x_ref[sl, :] + acc_ref[sl, :], g_ref[...], beta_ref[...])
            h_ref[sl, :] = h
            hb_ref[sl, :] = h.astype(hb_ref.dtype)
            return carry
        lax.fori_loop(0, acc_ref.shape[0] // LN_SLAB, body, 0)


def _out_proj_ln(a_out, b_out, w_out, x_part, g, beta, row0, n_total, prev, tm, tk):
    m, d = x_part.shape
    tm = min(tm, m)
    ka = a_out.shape[1] // tk
    kb = b_out.shape[1] // tk
    off = row0 // tm
    out_block = pl.BlockSpec((tm, d), lambda i, k: (i + off, 0), pipeline_mode=pl.Buffered(1))
    in_specs = [
        pl.BlockSpec((tm, tk), lambda i, k: (i + off, jnp.minimum(k, ka - 1))),
        pl.BlockSpec((tm, tk), lambda i, k: (i + off, jnp.maximum(k - ka, 0))),
        pl.BlockSpec((tk, d), lambda i, k: (k, 0)),
        pl.BlockSpec((tm, d), lambda i, k: (i, 0)),
        pl.BlockSpec((1, d), lambda i, k: (0, 0)),
        pl.BlockSpec((1, d), lambda i, k: (0, 0)),
    ]
    args = [a_out, b_out, w_out, x_part, g, beta]
    aliases = {}
    if prev is not None:
        aliases = {len(args): 0, len(args) + 1: 1}
        in_specs += [pl.BlockSpec(memory_space=pl.ANY)] * 2
        args += list(prev)
    return pl.pallas_call(
        functools.partial(_out_proj_ln_kernel, ka=ka),
        grid=(m // tm, ka + kb),
        in_specs=in_specs,
        out_specs=[out_block, out_block],
        out_shape=[jax.ShapeDtypeStruct((n_total, d), F32), jax.ShapeDtypeStruct((n_total, d), BF16)],
        scratch_shapes=[pltpu.VMEM((tm, d), F32)],
        input_output_aliases=aliases,
        compiler_params=_cparams(("parallel", "arbitrary")),
        name="out_proj_ln1",
    )(*args)


def _router_kernel(h_ref, w_ref, b_ref, o_ref):
    o_ref[...] = jnp.dot(h_ref[...], w_ref[...], precision=lax.Precision.HIGHEST,
                         preferred_element_type=F32) + b_ref[...]


def _router(h, w, b, tm):
    m, d = h.shape
    e = w.shape[1]
    return pl.pallas_call(
        _router_kernel,
        grid=(m // tm,),
        in_specs=[
            pl.BlockSpec((tm, d), lambda i: (i, 0)),
            pl.BlockSpec((d, e), lambda i: (0, 0)),
            pl.BlockSpec((1, e), lambda i: (0, 0)),
        ],
        out_specs=pl.BlockSpec((tm, e), lambda i: (i, 0)),
        out_shape=jax.ShapeDtypeStruct((m, e), F32),
        compiler_params=_cparams(("parallel",)),
        name="router",
    )(h, w, b)


def _ffn1_kernel(e_ref, j_ref, r_ref, n_ref, x_ref, wg_ref, wl_ref, bg_ref, bl_ref, o_ref):
    @pl.when(pl.program_id(0) < n_ref[0])
    def _():
        x = x_ref[...]
        hg = jnp.dot(x, wg_ref[...].astype(BF16), preferred_element_type=F32) + bg_ref[...]
        hl = jnp.dot(x, wl_ref[...].astype(BF16), preferred_element_type=F32) + bl_ref[...]
        hg = jnp.minimum(hg, SWIGLU_LIMIT)
        hl = jnp.clip(hl, -SWIGLU_LIMIT, SWIGLU_LIMIT)
        act = hg * (1.0 / (1.0 + jnp.exp(-SWIGLU_ALPHA * hg))) * (hl + 1.0)
        o_ref[...] = act.astype(o_ref.dtype)


def _ffn2_kernel(e_ref, j_ref, r_ref, n_ref, a_ref, w_ref, b_ref, g_ref, o_ref):
    @pl.when(pl.program_id(0) < n_ref[0])
    def _():
        y = jnp.dot(a_ref[...], w_ref[...].astype(BF16), preferred_element_type=F32) + b_ref[...]
        o_ref[...] = (y * g_ref[...]).astype(o_ref.dtype)


def _work_items(tiles_per, ncol, n_items_max):
    tile_end = jnp.cumsum(tiles_per)
    tile_start = tile_end - tiles_per
    items_per = tiles_per * ncol
    item_end = jnp.cumsum(items_per)
    n_items = item_end[-1:]
    g = jnp.minimum(jnp.arange(n_items_max, dtype=jnp.int32), n_items[0] - 1)
    e = jnp.minimum(jnp.searchsorted(item_end, g, side='right'), N_EXPERTS - 1).astype(jnp.int32)
    local = g - (item_end - items_per)[e]
    per = jnp.maximum(tiles_per[e], 1)
    return (e, (local // per).astype(jnp.int32), (tile_start[e] + local % per).astype(jnp.int32),
            n_items.astype(jnp.int32))


def _expert_ffn1(x_rows, w1, b1, items, tm, tn):
    n_rows, d = x_rows.shape
    dff = w1.shape[2] // 2
    ncol = dff // tn
    grid_spec = pltpu.PrefetchScalarGridSpec(
        num_scalar_prefetch=4,
        grid=(items[0].shape[0],),
        in_specs=[
            pl.BlockSpec((tm, d), lambda g, e, j, r, n: (r[g], 0)),
            pl.BlockSpec((None, d, tn), lambda g, e, j, r, n: (e[g], 0, j[g])),
            pl.BlockSpec((None, d, tn), lambda g, e, j, r, n: (e[g], 0, ncol + j[g])),
            pl.BlockSpec((None, 1, tn), lambda g, e, j, r, n: (e[g], 0, j[g])),
            pl.BlockSpec((None, 1, tn), lambda g, e, j, r, n: (e[g], 0, ncol + j[g])),
        ],
        out_specs=pl.BlockSpec((tm, tn), lambda g, e, j, r, n: (r[g], j[g])),
    )
    return pl.pallas_call(
        _ffn1_kernel,
        grid_spec=grid_spec,
        out_shape=jax.ShapeDtypeStruct((n_rows, dff), BF16),
        compiler_params=_cparams(("arbitrary",)),
        name="moe_ffn1",
    )(*items, x_rows, w1, w1, b1, b1)


def _expert_ffn2(act, w2, b2, row_gate, items, tm, tn):
    n_rows, dff = act.shape
    d = w2.shape[2]
    grid_spec = pltpu.PrefetchScalarGridSpec(
        num_scalar_prefetch=4,
        grid=(items[0].shape[0],),
        in_specs=[
            pl.BlockSpec((tm, dff), lambda g, e, j, r, n: (r[g], 0)),
            pl.BlockSpec((None, dff, tn), lambda g, e, j, r, n: (e[g], 0, j[g])),
            pl.BlockSpec((None, 1, tn), lambda g, e, j, r, n: (e[g], 0, j[g])),
            pl.BlockSpec((tm, 1), lambda g, e, j, r, n: (r[g], 0)),
        ],
        out_specs=pl.BlockSpec((tm, tn), lambda g, e, j, r, n: (r[g], j[g])),
    )
    return pl.pallas_call(
        _ffn2_kernel,
        grid_spec=grid_spec,
        out_shape=jax.ShapeDtypeStruct((n_rows, d), BF16),
        compiler_params=_cparams(("arbitrary",)),
        name="moe_ffn2",
    )(*items, act, w2, b2, row_gate)


def _residual_ln_kernel(h_ref, *refs):
    y_refs, (g_ref, b_ref, o_ref) = refs[:TOP_K], refs[TOP_K:]
    y = y_refs[0][...].astype(F32)
    for y_ref in y_refs[1:]:
        y = y + y_ref[...].astype(F32)
    o_ref[...] = _layer_norm(DEEPNORM_ALPHA * h_ref[...] + y, g_ref[...], b_ref[...])


def _residual_ln(h, ys, g, b, row0, m, tm):
    d = h.shape[1]
    tm = min(tm, m)
    off = row0 // tm
    tile = pl.BlockSpec((tm, d), lambda i: (i + off, 0))
    vec = pl.BlockSpec((1, d), lambda i: (0, 0))
    return pl.pallas_call(
        _residual_ln_kernel,
        grid=(m // tm,),
        in_specs=[tile] * (1 + TOP_K) + [vec, vec],
        out_specs=pl.BlockSpec((tm, d), lambda i: (i, 0)),
        out_shape=jax.ShapeDtypeStruct((m, d), F32),
        compiler_params=_cparams(("parallel",)),
        name="residual_ln2",
    )(h, *ys, g, b)


def _rel_bucket(rel):
    nb = REL_BUCKETS // 2
    max_exact = nb // 2
    ret = jnp.where(rel > 0, nb, 0)
    n = jnp.abs(rel)
    nf = jnp.maximum(n, 1).astype(F32)
    large = max_exact + (jnp.log(nf / max_exact) / math.log(REL_MAX_DIST / max_exact)
                         * (nb - max_exact)).astype(jnp.int32)
    large = jnp.minimum(large, nb - 1)
    return ret + jnp.where(n < max_exact, n, large)


def _bias_tables(rel_bias, tq):
    assert tq >= REL_MAX_DIST
    rel = 2 * tq - jnp.arange(4 * tq, dtype=jnp.int32)
    w = rel_bias[_rel_bucket(rel)].astype(F32).T[:, None, :]
    far = rel_bias[_rel_bucket(jnp.array([-(tq + 1), tq + 1], jnp.int32))].astype(F32)
    return w, far


def _rope_tables(seq):
    tok = jnp.arange(seq, dtype=jnp.int32)
    row_id = (tok // GRID_W).astype(F32)
    col_id = (tok % GRID_W).astype(F32)
    inv = ROPE_THETA ** (-jnp.arange(0, ROPE_AXIS_DIM, 2, dtype=F32) / ROPE_AXIS_DIM)
    ang_r = row_id[:, None] * inv[None, :]
    ang_c = col_id[:, None] * inv[None, :]
    cos = jnp.concatenate([jnp.cos(ang_r)] * 2 + [jnp.cos(ang_c)] * 2, axis=-1)
    sin = jnp.concatenate([-jnp.sin(ang_r), jnp.sin(ang_r), -jnp.sin(ang_c), jnp.sin(ang_c)], axis=-1)
    return cos, sin


def _route(logits, tm, n_tiles):
    n = logits.shape[0]
    top_val, top_idx = lax.top_k(logits, TOP_K)
    gate = jax.nn.softmax(top_val, axis=-1)
    n_assign = n * TOP_K
    e_flat = top_idx.reshape(-1)
    g_flat = gate.reshape(-1)
    experts = jnp.arange(N_EXPERTS, dtype=jnp.int32)
    counts = jnp.sum((e_flat[:, None] == experts[None, :]).astype(jnp.int32), axis=0)
    tiles_per = (counts + tm - 1) // tm
    tile_end = jnp.cumsum(tiles_per)
    pstart = (tile_end - tiles_per) * tm
    start = jnp.cumsum(counts) - counts
    order = jnp.argsort(e_flat).astype(jnp.int32)
    rank = jnp.argsort(order).astype(jnp.int32)
    dest = (pstart[e_flat] + rank - start[e_flat]).reshape(n, TOP_K)
    r = jnp.arange(n_tiles * tm, dtype=jnp.int32)
    e_row = jnp.minimum(jnp.searchsorted(tile_end, r // tm, side='right'), N_EXPERTS - 1)
    idx = r - pstart[e_row]
    real = idx < counts[e_row]
    src = order[jnp.clip(start[e_row] + idx, 0, n_assign - 1)]
    row_tok = jnp.where(real, src // TOP_K, 0)
    row_gate = jnp.where(real, g_flat[src], 0.0)
    return row_tok, row_gate, dest, tiles_per


PROJ_TM, PROJ_TN = 1024, 1024
PREP_TM = 512
ATTN_TQ = 512
GQA_HEADS_PER_STEP = 4
OUT_TM, OUT_TK = 512, 512
LN_SLAB = 64
ROUTER_TM = 512
MOE_TM, MOE_TN = 512, 512
LN_TM = 256


def kernel(x_prompt, x_sample, meta_tokens, rel_bias, w_in, lambda_q1, lambda_k1, lambda_q2,
           lambda_k2, a_subln, q_norm, k_norm, w_out, ln1_g, ln1_b, w_router, b_router,
           w1, b1, w2, b2, ln2_g, ln2_b):
    d = x_prompt.shape[-1]
    seq = x_prompt.shape[1]
    assert x_sample.shape[1] == seq and seq % GRID_W == 0
    nb_p, nb_s = x_prompt.shape[0], x_sample.shape[0]
    nb = nb_p + nb_s
    x_p, x_s = x_prompt.reshape(-1, d), x_sample.reshape(-1, d)
    n_p = x_p.shape[0]
    n = n_p + x_s.shape[0]
    x_b = jnp.concatenate([x_p.astype(BF16), x_s.astype(BF16)], axis=0)
    tq = min(ATTN_TQ, seq)

    w_in_b = w_in[0].astype(BF16)
    colscale = jnp.concatenate([jnp.full((A_WIDTH,), QUERY_SCALE, F32),
                                jnp.ones((IN_COLS - A_WIDTH,), F32)])[None]
    tn = min(PROJ_TN, IN_COLS)
    proj = _projection(x_b, w_in_b, colscale, PROJ_TM, tn)
    meta_rows = jnp.zeros((META_PAD, d), F32).at[:N_META].set(meta_tokens)
    meta_proj = _projection(meta_rows.astype(BF16), w_in_b, colscale, META_PAD, tn)

    gains = jnp.concatenate([jnp.tile(q_norm[0] * QUERY_SCALE, B_HEADS),
                             jnp.tile(k_norm[0], B_KV_HEADS)]).reshape(-1, 1, B_KV_WIDTH)
    cos, sin = _rope_tables(seq)
    bqk = _qk_prep(proj, gains, cos, sin, min(PREP_TM, seq))
    meta_bqk = _qk_prep(meta_proj, gains, jnp.ones((META_PAD, HEAD_DIM), F32),
                        jnp.zeros((META_PAD, HEAD_DIM), F32), META_PAD)

    lam = (jnp.exp(jnp.sum(lambda_q1[0].astype(F32) * lambda_k1[0].astype(F32)))
           - jnp.exp(jnp.sum(lambda_q2[0].astype(F32) * lambda_k2[0].astype(F32))) + LAMBDA_INIT)
    w_bias, far = _bias_tables(rel_bias * LOG2E, tq)
    consts = jnp.concatenate([far[0], far[1], lam[None]]).astype(F32)
    proj3 = proj.reshape(nb, seq, IN_COLS)
    a_v0, b_v0 = 2 * A_WIDTH, 3 * A_WIDTH + B_WIDTH + B_KV_WIDTH
    vt_a, meta_vt_a = _chunked_transpose(proj3[:, :, a_v0:a_v0 + A_WIDTH],
                                         meta_proj[:, a_v0:a_v0 + A_WIDTH], A_HEADS, 2 * HEAD_DIM, tq)
    vt_b, meta_vt_b = _chunked_transpose(proj3[:, :, b_v0:], meta_proj[:, b_v0:],
                                         B_KV_HEADS, HEAD_DIM, tq)
    a_out = _diff_attention(proj3, vt_a, meta_proj, meta_vt_a, w_bias, consts,
                            a_subln.astype(F32).reshape(-1, 1), tq)
    b_out = _gqa_attention(bqk.reshape(nb, seq, -1), vt_b, meta_bqk, meta_vt_b, tq,
                           GQA_HEADS_PER_STEP)

    a2, b2d, w_out_b = a_out.reshape(n, A_WIDTH), b_out.reshape(n, B_WIDTH), w_out[0].astype(BF16)
    first = _out_proj_ln(a2, b2d, w_out_b, x_p, ln1_g, ln1_b, 0, n, None, OUT_TM, OUT_TK)
    h1, h1_b = _out_proj_ln(a2, b2d, w_out_b, x_s, ln1_g, ln1_b, n_p, n, first, OUT_TM, OUT_TK)

    logits = _router(h1, w_router[0], b_router, min(ROUTER_TM, n))
    tm = min(MOE_TM, n)
    n_tiles = (n * TOP_K + N_EXPERTS * (tm - 1)) // tm
    row_tok, row_gate, dest, tiles_per = _route(logits, tm, n_tiles)
    x_rows = h1_b[row_tok]
    dff = w2.shape[2]
    tn1, tn2 = min(MOE_TN, dff), min(MOE_TN, d)
    act = _expert_ffn1(x_rows, w1[0], b1[0].reshape(N_EXPERTS, 1, -1),
                       _work_items(tiles_per, dff // tn1, n_tiles * (dff // tn1)), tm, tn1)
    y_rows = _expert_ffn2(act, w2[0], b2[0].reshape(N_EXPERTS, 1, -1), row_gate[:, None],
                          _work_items(tiles_per, d // tn2, n_tiles * (d // tn2)), tm, tn2)
    ys = [y_rows[dest[:, k]] for k in range(TOP_K)]

    out_p = _residual_ln(h1, ys, ln2_g, ln2_b, 0, n_p, LN_TM)
    out_s = _residual_ln(h1, ys, ln2_g, ln2_b, n_p, n - n_p, LN_TM)
    return (out_p.reshape(nb_p, seq, d), out_s.reshape(nb_s, seq, d))
```

```python
import functools
import math

import jax
import jax.numpy as jnp
from jax import lax
from jax.experimental import pallas as pl
from jax.experimental.pallas import tpu as pltpu

HEAD_DIM = 128
N_META = 16
GRID_W = 64
A_HEADS = 8
A_WIDTH = A_HEADS * 2 * HEAD_DIM
B_HEADS = 16
B_KV_HEADS = 4
B_GROUP = B_HEADS // B_KV_HEADS
B_WIDTH = B_HEADS * HEAD_DIM
B_KV_WIDTH = B_KV_HEADS * HEAD_DIM
IN_COLS = 3 * A_WIDTH + B_WIDTH + 2 * B_KV_WIDTH
ROPE_THETA = 10000.0
ROPE_AXIS_DIM = HEAD_DIM // 2
REL_BUCKETS = 32
REL_MAX_DIST = 128
N_EXPERTS = 32
TOP_K = 4
SWIGLU_ALPHA = 1.702
SWIGLU_LIMIT = 7.0
DEPTH = 1
DEEPNORM_ALPHA = (2 * DEPTH) ** 0.25
LN_EPS = 1e-5
RMS_EPS = 1e-6
LAMBDA_INIT = 0.8 - 0.6 * math.exp(-0.3 * 0)
LOG2E = math.log2(math.e)
QUERY_SCALE = HEAD_DIM ** -0.5 * LOG2E

LANES = 128
META_PAD = 128
MASK_VALUE = -1e30
VMEM_LIMIT = 56 * 1024 * 1024

BF16 = jnp.bfloat16
F32 = jnp.float32


def _cparams(sem):
    return pltpu.CompilerParams(dimension_semantics=sem, vmem_limit_bytes=VMEM_LIMIT)


def _proj_kernel(a_ref, b_ref, s_ref, o_ref):
    acc = jnp.dot(a_ref[...], b_ref[...], preferred_element_type=F32)
    o_ref[...] = (acc * s_ref[...]).astype(o_ref.dtype)


def _projection(a, b, colscale, tm, tn):
    m, k = a.shape
    n = b.shape[1]
    tm = min(tm, m)
    return pl.pallas_call(
        _proj_kernel,
        grid=(m // tm, n // tn),
        in_specs=[
            pl.BlockSpec((tm, k), lambda i, j: (i, 0)),
            pl.BlockSpec((k, tn), lambda i, j: (0, j)),
            pl.BlockSpec((1, tn), lambda i, j: (0, j)),
        ],
        out_specs=pl.BlockSpec((tm, tn), lambda i, j: (i, j)),
        out_shape=jax.ShapeDtypeStruct((m, n), BF16),
        compiler_params=_cparams(("parallel", "parallel")),
        name="in_proj",
    )(a, b, colscale)


def _qk_prep_kernel(x_ref, g_ref, c_ref, s_ref, o_ref):
    cos = c_ref[...]
    sin = s_ref[...]
    lane = lax.broadcasted_iota(jnp.int32, cos.shape, 1)
    first_half = (lane % (ROPE_AXIS_DIM)) < (ROPE_AXIS_DIM // 2)
    for hh in range(x_ref.shape[1] // HEAD_DIM):
        sl = slice(hh * HEAD_DIM, (hh + 1) * HEAD_DIM)
        x = x_ref[:, sl].astype(F32)
        ms = jnp.mean(x * x, axis=-1, keepdims=True)
        y = x * lax.rsqrt(ms + RMS_EPS) * g_ref[:, sl]
        partner = jnp.where(first_half,
                            pltpu.roll(y, HEAD_DIM - ROPE_AXIS_DIM // 2, 1),
                            pltpu.roll(y, ROPE_AXIS_DIM // 2, 1))
        o_ref[:, sl] = (y * cos + partner * sin).astype(o_ref.dtype)


def _qk_prep(proj, gains, cos, sin, tm):
    m = proj.shape[0]
    tm = min(tm, m)
    w = B_KV_WIDTH
    first = (3 * A_WIDTH) // w
    nblk = (B_WIDTH + B_KV_WIDTH) // w
    ntab = cos.shape[0] // tm
    return pl.pallas_call(
        _qk_prep_kernel,
        grid=(m // tm, nblk),
        in_specs=[
            pl.BlockSpec((tm, w), lambda i, j: (i, first + j)),
            pl.BlockSpec((None, 1, w), lambda i, j: (j, 0, 0)),
            pl.BlockSpec((tm, HEAD_DIM), lambda i, j: (i % ntab, 0)),
            pl.BlockSpec((tm, HEAD_DIM), lambda i, j: (i % ntab, 0)),
        ],
        out_specs=pl.BlockSpec((tm, w), lambda i, j: (i, j)),
        out_shape=jax.ShapeDtypeStruct((m, B_WIDTH + B_KV_WIDTH), BF16),
        compiler_params=_cparams(("parallel", "parallel")),
        name="qk_prep",
    )(proj, gains, cos, sin)


def _scores_t(k, q):
    return lax.dot_general(k, q, (((1,), (1,)), ((), ())), preferred_element_type=F32)


def _softmax_init(s, v_t, m_ref, l_ref, acc_ref):
    m = jnp.max(s, axis=0, keepdims=True)
    p = jnp.exp2(s - m)
    m_ref[...] = m
    l_ref[...] = jnp.sum(p, axis=0, keepdims=True)
    acc_ref[...] = jnp.dot(v_t, p.astype(v_t.dtype), preferred_element_type=F32)


def _softmax_update(s, shift, v_t, m_ref, l_ref, acc_ref):
    m_prev = m_ref[...]
    m_new = jnp.maximum(m_prev, jnp.max(s, axis=0, keepdims=True) + shift)
    p = jnp.exp2(s - (m_new - shift))
    alpha = jnp.exp2(m_prev - m_new)
    l_ref[...] = alpha * l_ref[...] + jnp.sum(p, axis=0, keepdims=True)
    acc_ref[...] = alpha * acc_ref[...] + jnp.dot(v_t, p.astype(v_t.dtype),
                                                  preferred_element_type=F32)
    m_ref[...] = m_new


def _diff_attn_kernel(c_ref, q1_ref, q2_ref, k1_ref, k2_ref, vt_ref, mk1_ref, mk2_ref, mvt_ref,
                      w_ref, g_ref, o_ref, band_ref, mbias_ref,
                      m1_ref, l1_ref, acc1_ref, m2_ref, l2_ref, acc2_ref, *, tk):
    h = pl.program_id(1)
    i = pl.program_id(2)
    tq = q1_ref.shape[0]
    nk = k1_ref.shape[0] // tk
    bias_left = c_ref[h]
    bias_right = c_ref[A_HEADS + h]
    lam = c_ref[2 * A_HEADS]
    qs = (q1_ref[...], q2_ref[...])
    krefs = (k1_ref, k2_ref)
    stats = ((m1_ref, l1_ref, acc1_ref), (m2_ref, l2_ref, acc2_ref))

    @pl.when(i == 0)
    def _():
        width = w_ref.shape[1]
        rolled = pltpu.roll(jnp.broadcast_to(w_ref[...], (tk, width)), 0, 1,
                            stride=1, stride_axis=0)
        for d in range(3):
            band_ref[d] = rolled[:, (3 - d) * tq:(4 - d) * tq]
        mrolled = pltpu.roll(jnp.broadcast_to(w_ref[...], (META_PAD, width)), width - N_META, 1,
                             stride=1, stride_axis=0)
        mbias_ref[...] = mrolled[:, 2 * tq:3 * tq]

    meta_ok = lax.broadcasted_iota(jnp.int32, mbias_ref.shape, 0) < N_META
    meta_bias = jnp.where(i == 0, mbias_ref[...], bias_left)
    mvt = mvt_ref[...]
    for a, mk_ref in enumerate((mk1_ref, mk2_ref)):
        s = jnp.where(meta_ok, _scores_t(mk_ref[...], qs[a]) + meta_bias, MASK_VALUE)
        _softmax_init(s, mvt, *stats[a])

    def rows(j):
        return pl.ds(pl.multiple_of(j * tk, tk), tk)

    def const_step(shift):
        def body(j, carry):
            vt = vt_ref[j]
            for a in range(2):
                _softmax_update(_scores_t(krefs[a][rows(j), :], qs[a]), shift, vt, *stats[a])
            return carry
        return body

    def band_step(j, d):
        vt = vt_ref[j]
        for a in range(2):
            s = _scores_t(krefs[a][rows(j), :], qs[a]) + band_ref[d]
            _softmax_update(s, 0.0, vt, *stats[a])

    lax.fori_loop(0, jnp.maximum(i - 1, 0), const_step(bias_left), 0)

    @pl.when(i >= 1)
    def _():
        band_step(i - 1, 0)

    band_step(i, 1)

    @pl.when(i + 1 < nk)
    def _():
        band_step(i + 1, 2)

    lax.fori_loop(jnp.minimum(i + 2, nk), nk, const_step(bias_right), 0)

    o = acc1_ref[...] * (1.0 / l1_ref[...]) - lam * (acc2_ref[...] * (1.0 / l2_ref[...]))
    ms = jnp.mean(o * o, axis=0, keepdims=True)
    o = o * lax.rsqrt(ms + RMS_EPS) * g_ref[...] * (1.0 - LAMBDA_INIT)
    o_ref[...] = o.T.astype(o_ref.dtype)


def _diff_attention(proj3, vt, meta_proj, meta_vt, w_bias, consts, subln, tq):
    nb, seq, _ = proj3.shape
    dv = 2 * HEAD_DIM
    nk = seq // tq
    kq2 = A_HEADS
    kk1 = A_WIDTH // HEAD_DIM
    kk2 = kk1 + A_HEADS
    stat = [pltpu.VMEM((1, tq), F32), pltpu.VMEM((1, tq), F32), pltpu.VMEM((dv, tq), F32)]
    return pl.pallas_call(
        functools.partial(_diff_attn_kernel, tk=tq),
        grid=(nb, A_HEADS, nk),
        in_specs=[
            pl.BlockSpec(memory_space=pltpu.SMEM),
            pl.BlockSpec((None, tq, HEAD_DIM), lambda b, h, i: (b, i, h)),
            pl.BlockSpec((None, tq, HEAD_DIM), lambda b, h, i: (b, i, kq2 + h)),
            pl.BlockSpec((None, seq, HEAD_DIM), lambda b, h, i: (b, 0, kk1 + h)),
            pl.BlockSpec((None, seq, HEAD_DIM), lambda b, h, i: (b, 0, kk2 + h)),
            pl.BlockSpec((None, None, nk, dv, tq), lambda b, h, i: (b, h, 0, 0, 0)),
            pl.BlockSpec((META_PAD, HEAD_DIM), lambda b, h, i: (0, kk1 + h)),
            pl.BlockSpec((META_PAD, HEAD_DIM), lambda b, h, i: (0, kk2 + h)),
            pl.BlockSpec((None, dv, META_PAD), lambda b, h, i: (h, 0, 0)),
            pl.BlockSpec((None, 1, 4 * tq), lambda b, h, i: (h, 0, 0)),
            pl.BlockSpec((dv, 1), lambda b, h, i: (0, 0)),
        ],
        out_specs=pl.BlockSpec((None, tq, dv), lambda b, h, i: (b, i, h)),
        out_shape=jax.ShapeDtypeStruct((nb, seq, A_WIDTH), BF16),
        scratch_shapes=[pltpu.VMEM((3, tq, tq), F32), pltpu.VMEM((META_PAD, tq), F32)] + stat + stat,
        compiler_params=_cparams(("parallel", "parallel", "arbitrary")),
        name="diff_attn",
    )(consts, proj3, proj3, proj3, proj3, vt, meta_proj, meta_proj, meta_vt, w_bias, subln)


def _gqa_kernel(q_ref, k_ref, vt_ref, mk_ref, mvt_ref, o_ref, *stat_refs, tk, heads):
    nk = k_ref.shape[0] // tk
    tq = q_ref.shape[0]
    qs = [q_ref[:, g * HEAD_DIM:(g + 1) * HEAD_DIM] for g in range(heads)]
    stats = [stat_refs[3 * g:3 * g + 3] for g in range(heads)]
    meta_ok = lax.broadcasted_iota(jnp.int32, (META_PAD, tq), 0) < N_META
    mk = mk_ref[...]
    mvt = mvt_ref[...]
    for g in range(heads):
        s = jnp.where(meta_ok, _scores_t(mk, qs[g]), MASK_VALUE)
        _softmax_init(s, mvt, *stats[g])

    def body(j, carry):
        k = k_ref[pl.ds(pl.multiple_of(j * tk, tk), tk), :]
        vt = vt_ref[j]
        for g in range(heads):
            _softmax_update(_scores_t(k, qs[g]), 0.0, vt, *stats[g])
        return carry

    lax.fori_loop(0, nk, body, 0)
    for g in range(heads):
        m_ref, l_ref, acc_ref = stats[g]
        o = acc_ref[...] * (1.0 / l_ref[...])
        o_ref[:, g * HEAD_DIM:(g + 1) * HEAD_DIM] = o.T.astype(o_ref.dtype)


def _gqa_attention(bqk3, vt, meta_bqk, meta_vt, tq, heads):
    nb, seq, _ = bqk3.shape
    nk = seq // tq
    kk = B_HEADS
    per_group = B_GROUP // heads
    stat = [pltpu.VMEM((1, tq), F32), pltpu.VMEM((1, tq), F32), pltpu.VMEM((HEAD_DIM, tq), F32)]
    return pl.pallas_call(
        functools.partial(_gqa_kernel, tk=tq, heads=heads),
        grid=(nb, B_HEADS // heads, nk),
        in_specs=[
            pl.BlockSpec((None, tq, heads * HEAD_DIM), lambda b, h, i: (b, i, h)),
            pl.BlockSpec((None, seq, HEAD_DIM), lambda b, h, i: (b, 0, kk + h // per_group)),
            pl.BlockSpec((None, None, nk, HEAD_DIM, tq),
                         lambda b, h, i: (b, h // per_group, 0, 0, 0)),
            pl.BlockSpec((META_PAD, HEAD_DIM), lambda b, h, i: (0, kk + h // per_group)),
            pl.BlockSpec((None, HEAD_DIM, META_PAD), lambda b, h, i: (h // per_group, 0, 0)),
        ],
        out_specs=pl.BlockSpec((None, tq, heads * HEAD_DIM), lambda b, h, i: (b, i, h)),
        out_shape=jax.ShapeDtypeStruct((nb, seq, B_WIDTH), BF16),
        scratch_shapes=stat * heads,
        compiler_params=_cparams(("parallel", "parallel", "parallel")),
        name="gqa_attn",
    )(bqk3, bqk3, vt, meta_bqk, meta_vt)


def _chunked_transpose(v3, meta_v, heads, dv, tk):
    nb, seq, _ = v3.shape
    vt = v3.reshape(nb, seq // tk, tk, heads, dv).transpose(0, 3, 1, 4, 2)
    meta_vt = meta_v.reshape(META_PAD, heads, dv).transpose(1, 2, 0)
    return vt, meta_vt


def _layer_norm(y, g, b):
    mu = jnp.mean(y, axis=-1, keepdims=True)
    yc = y - mu
    var = jnp.mean(yc * yc, axis=-1, keepdims=True)
    return yc * lax.rsqrt(var + LN_EPS) * g + b


def _out_proj_ln_kernel(a_ref, b_ref, w_ref, xp_hbm, xs_hbm, g_ref, beta_ref, h_ref, hb_ref,
                        acc_ref, x_ref, sem, *, ka, tiles_p):
    i = pl.program_id(0)
    k = pl.program_id(1)
    tm = acc_ref.shape[0]

    def x_copy(src, tile):
        rows = pl.ds(pl.multiple_of(tile * tm, tm), tm)
        return pltpu.make_async_copy(src.at[rows, :], x_ref, sem.at[0])

    def for_x_source(fn):
        @pl.when(i < tiles_p)
        def _():
            fn(x_copy(xp_hbm, i))

        @pl.when(i >= tiles_p)
        def _():
            fn(x_copy(xs_hbm, i - tiles_p))

    @pl.when(k == 0)
    def _():
        acc_ref[...] = jnp.zeros_like(acc_ref)
        for_x_source(lambda c: c.start())

    @pl.when(k < ka)
    def _():
        acc_ref[...] += jnp.dot(a_ref[...], w_ref[...], preferred_element_type=F32)

    @pl.when(k >= ka)
    def _():
        acc_ref[...] += jnp.dot(b_ref[...], w_ref[...], preferred_element_type=F32)

    @pl.when(k == pl.num_programs(1) - 1)
    def _():
        for_x_source(lambda c: c.wait())

        def body(c, carry):
            sl = pl.ds(pl.multiple_of(c * LN_SLAB, LN_SLAB), LN_SLAB)
            h = _layer_norm(DEEPNORM_ALPHA * x_ref[sl, :] + acc_ref[sl, :], g_ref[...], beta_ref[...])
            h_ref[sl, :] = h
            hb_ref[sl, :] = h.astype(hb_ref.dtype)
            return carry
        lax.fori_loop(0, acc_ref.shape[0] // LN_SLAB, body, 0)


def _out_proj_ln(a_out, b_out, w_out, x_p, x_s, g, beta, tm, tk):
    n_p, d = x_p.shape
    n = n_p + x_s.shape[0]
    tm = math.gcd(math.gcd(tm, n_p), x_s.shape[0])
    ka = a_out.shape[1] // tk
    kb = b_out.shape[1] // tk
    out_block = pl.BlockSpec((tm, d), lambda i, k: (i, 0), pipeline_mode=pl.Buffered(1))
    return pl.pallas_call(
        functools.partial(_out_proj_ln_kernel, ka=ka, tiles_p=n_p // tm),
        grid=(n // tm, ka + kb),
        in_specs=[
            pl.BlockSpec((tm, tk), lambda i, k: (i, jnp.minimum(k, ka - 1))),
            pl.BlockSpec((tm, tk), lambda i, k: (i, jnp.maximum(k - ka, 0))),
            pl.BlockSpec((tk, d), lambda i, k: (k, 0)),
            pl.BlockSpec(memory_space=pl.ANY),
            pl.BlockSpec(memory_space=pl.ANY),
            pl.BlockSpec((1, d), lambda i, k: (0, 0)),
            pl.BlockSpec((1, d), lambda i, k: (0, 0)),
        ],
        out_specs=[out_block, out_block],
        out_shape=[jax.ShapeDtypeStruct((n, d), F32), jax.ShapeDtypeStruct((n, d), BF16)],
        scratch_shapes=[pltpu.VMEM((tm, d), F32), pltpu.VMEM((tm, d), F32),
                        pltpu.SemaphoreType.DMA((1,))],
        compiler_params=_cparams(("parallel", "arbitrary")),
        name="out_proj_ln1",
    )(a_out, b_out, w_out, x_p, x_s, g, beta)


def _router_kernel(h_ref, w_ref, b_ref, o_ref):
    o_ref[...] = jnp.dot(h_ref[...], w_ref[...], precision=lax.Precision.HIGHEST,
                         preferred_element_type=F32) + b_ref[...]


def _router(h, w, b, tm):
    m, d = h.shape
    e = w.shape[1]
    return pl.pallas_call(
        _router_kernel,
        grid=(m // tm,),
        in_specs=[
            pl.BlockSpec((tm, d), lambda i: (i, 0)),
            pl.BlockSpec((d, e), lambda i: (0, 0)),
            pl.BlockSpec((1, e), lambda i: (0, 0)),
        ],
        out_specs=pl.BlockSpec((tm, e), lambda i: (i, 0)),
        out_shape=jax.ShapeDtypeStruct((m, e), F32),
        compiler_params=_cparams(("parallel",)),
        name="router",
    )(h, w, b)


class _Items:
    EXPERT, COL, ROW, IN_ROW, GROUP, FIRST, NEXT_EXPERT, NEXT_COL, HAS_NEXT, COUNT = range(10)
    N = 10


def _stream_weights(s, w_hbm, col_offsets, wbuf, sem, tn):
    g = pl.program_id(0)

    def copies(e, j, slot):
        return [pltpu.make_async_copy(
            w_hbm.at[e, :, pl.ds(pl.multiple_of((off + j) * tn, tn), tn)],
            wbuf.at[slot, c], sem.at[slot, c]) for c, off in enumerate(col_offsets)]

    slot = s[_Items.GROUP][g] % 2

    @pl.when(g == 0)
    def _():
        for c in copies(s[_Items.EXPERT][0], s[_Items.COL][0], 0):
            c.start()

    @pl.when(s[_Items.FIRST][g] == 1)
    def _():
        for c in copies(s[_Items.EXPERT][g], s[_Items.COL][g], slot):
            c.wait()

        @pl.when(s[_Items.HAS_NEXT][g] == 1)
        def _():
            for c in copies(s[_Items.NEXT_EXPERT][g], s[_Items.NEXT_COL][g], 1 - slot):
                c.start()

    return slot


def _ffn1_kernel(*refs, ncol, tn):
    s = refs[:_Items.N]
    x_ref, w_hbm, bg_ref, bl_ref, o_ref, wbuf, sem = refs[_Items.N:]

    @pl.when(pl.program_id(0) < s[_Items.COUNT][0])
    def _():
        slot = _stream_weights(s, w_hbm, (0, ncol), wbuf, sem, tn)
        x = x_ref[...]
        hg = jnp.dot(x, wbuf[slot, 0].astype(BF16), preferred_element_type=F32) + bg_ref[...]
        hl = jnp.dot(x, wbuf[slot, 1].astype(BF16), preferred_element_type=F32) + bl_ref[...]
        hg = jnp.minimum(hg, SWIGLU_LIMIT)
        hl = jnp.clip(hl, -SWIGLU_LIMIT, SWIGLU_LIMIT)
        act = hg * (1.0 / (1.0 + jnp.exp(-SWIGLU_ALPHA * hg))) * (hl + 1.0)
        o_ref[...] = act.astype(o_ref.dtype)

    @pl.when(pl.program_id(0) >= s[_Items.COUNT][0])
    def _():
        o_ref[...] = jnp.zeros_like(o_ref)


def _ffn2_kernel(*refs, tn):
    s = refs[:_Items.N]
    a_ref, w_hbm, b_ref, g_ref, o_ref, wbuf, sem = refs[_Items.N:]

    @pl.when(pl.program_id(0) < s[_Items.COUNT][0])
    def _():
        slot = _stream_weights(s, w_hbm, (0,), wbuf, sem, tn)
        y = jnp.dot(a_ref[...], wbuf[slot, 0].astype(BF16), preferred_element_type=F32) + b_ref[...]
        o_ref[...] = (y * g_ref[...]).astype(o_ref.dtype)

    @pl.when(pl.program_id(0) >= s[_Items.COUNT][0])
    def _():
        o_ref[...] = jnp.zeros_like(o_ref)


def _work_items(tiles_per, ncol, n_items_max):
    i32 = jnp.int32
    experts = jnp.arange(N_EXPERTS, dtype=i32)
    tile_end = jnp.cumsum(tiles_per)
    tile_start = tile_end - tiles_per
    items_per = tiles_per * ncol
    item_end = jnp.cumsum(items_per)
    n_items = item_end[-1:]
    g_all = jnp.arange(n_items_max, dtype=i32)
    fill = g_all - n_items[0]
    g = jnp.minimum(g_all, n_items[0] - 1)
    e = jnp.minimum(jnp.sum((item_end[None, :] <= g[:, None]).astype(i32), axis=1), N_EXPERTS - 1)
    local = g - (item_end - items_per)[e]
    per = jnp.maximum(tiles_per[e], 1)
    j = local // per
    t = local % per
    used = tiles_per > 0
    groups_before = (jnp.cumsum(used.astype(i32)) - used.astype(i32)) * ncol
    later = jnp.where(used[None, :] & (experts[None, :] > experts[:, None]), experts[None, :], N_EXPERTS)
    next_used = jnp.min(later, axis=1)
    last_col = j == ncol - 1
    next_e = jnp.where(last_col, next_used[e], e)
    has_next = next_e < N_EXPERTS
    row = tile_start[e] + t
    return tuple(a.astype(i32) for a in (
        e, jnp.where(fill >= 0, fill % ncol, j), jnp.where(fill >= 0, tile_end[-1] + fill // ncol, row),
        row, groups_before[e] + j, t == 0,
        jnp.minimum(next_e, N_EXPERTS - 1), jnp.where(last_col, 0, j + 1), has_next, n_items))


def _expert_ffn1(x_rows, w1, b1, items, tm, tn):
    n_rows, d = x_rows.shape
    dff = w1.shape[2] // 2
    ncol = dff // tn
    E, C, R = _Items.EXPERT, _Items.COL, _Items.ROW
    grid_spec = pltpu.PrefetchScalarGridSpec(
        num_scalar_prefetch=_Items.N,
        grid=(items[0].shape[0],),
        in_specs=[
            pl.BlockSpec((tm, d), lambda g, *s: (s[_Items.IN_ROW][g], 0)),
            pl.BlockSpec(memory_space=pl.ANY),
            pl.BlockSpec((None, 1, tn), lambda g, *s: (s[E][g], 0, s[C][g])),
            pl.BlockSpec((None, 1, tn), lambda g, *s: (s[E][g], 0, ncol + s[C][g])),
        ],
        out_specs=pl.BlockSpec((tm, tn), lambda g, *s: (s[R][g], s[C][g])),
        scratch_shapes=[pltpu.VMEM((2, 2, d, tn), F32), pltpu.SemaphoreType.DMA((2, 2))],
    )
    return pl.pallas_call(
        functools.partial(_ffn1_kernel, ncol=ncol, tn=tn),
        grid_spec=grid_spec,
        out_shape=jax.ShapeDtypeStruct((n_rows, dff), BF16),
        compiler_params=_cparams(("arbitrary",)),
        name="moe_ffn1",
    )(*items, x_rows, w1, b1, b1)


def _expert_ffn2(act, w2, b2, row_gate, items, tm, tn):
    n_rows, dff = act.shape
    d = w2.shape[2]
    E, C, R = _Items.EXPERT, _Items.COL, _Items.ROW
    grid_spec = pltpu.PrefetchScalarGridSpec(
        num_scalar_prefetch=_Items.N,
        grid=(items[0].shape[0],),
        in_specs=[
            pl.BlockSpec((tm, dff), lambda g, *s: (s[_Items.IN_ROW][g], 0)),
            pl.BlockSpec(memory_space=pl.ANY),
            pl.BlockSpec((None, 1, tn), lambda g, *s: (s[E][g], 0, s[C][g])),
            pl.BlockSpec((tm, 1), lambda g, *s: (s[_Items.IN_ROW][g], 0)),
        ],
        out_specs=pl.BlockSpec((tm, tn), lambda g, *s: (s[R][g], s[C][g])),
        scratch_shapes=[pltpu.VMEM((2, 1, dff, tn), F32), pltpu.SemaphoreType.DMA((2, 1))],
    )
    return pl.pallas_call(
        functools.partial(_ffn2_kernel, tn=tn),
        grid_spec=grid_spec,
        out_shape=jax.ShapeDtypeStruct((n_rows, d), BF16),
        compiler_params=_cparams(("arbitrary",)),
        name="moe_ffn2",
    )(*items, act, w2, b2, row_gate)


def _residual_ln_kernel(h_ref, *refs):
    y_refs, (g_ref, b_ref, o_ref) = refs[:TOP_K], refs[TOP_K:]
    y = y_refs[0][...].astype(F32)
    for y_ref in y_refs[1:]:
        y = y + y_ref[...].astype(F32)
    o_ref[...] = _layer_norm(DEEPNORM_ALPHA * h_ref[...] + y, g_ref[...], b_ref[...])


def _residual_ln(h, ys, g, b, row0, m, tm):
    d = h.shape[1]
    tm = min(tm, m)
    off = row0 // tm
    tile = pl.BlockSpec((tm, d), lambda i: (i + off, 0))
    vec = pl.BlockSpec((1, d), lambda i: (0, 0))
    return pl.pallas_call(
        _residual_ln_kernel,
        grid=(m // tm,),
        in_specs=[tile] * (1 + TOP_K) + [vec, vec],
        out_specs=pl.BlockSpec((tm, d), lambda i: (i, 0)),
        out_shape=jax.ShapeDtypeStruct((m, d), F32),
        compiler_params=_cparams(("parallel",)),
        name="residual_ln2",
    )(h, *ys, g, b)


def _rel_bucket(rel):
    nb = REL_BUCKETS // 2
    max_exact = nb // 2
    ret = jnp.where(rel > 0, nb, 0)
    n = jnp.abs(rel)
    nf = jnp.maximum(n, 1).astype(F32)
    large = max_exact + (jnp.log(nf / max_exact) / math.log(REL_MAX_DIST / max_exact)
                         * (nb - max_exact)).astype(jnp.int32)
    large = jnp.minimum(large, nb - 1)
    return ret + jnp.where(n < max_exact, n, large)


def _bias_tables(rel_bias, tq):
    assert tq >= REL_MAX_DIST
    rel = 2 * tq - jnp.arange(4 * tq, dtype=jnp.int32)
    w = rel_bias[_rel_bucket(rel)].astype(F32).T[:, None, :]
    far = rel_bias[_rel_bucket(jnp.array([-(tq + 1), tq + 1], jnp.int32))].astype(F32)
    return w, far


def _rope_tables(seq):
    tok = jnp.arange(seq, dtype=jnp.int32)
    row_id = (tok // GRID_W).astype(F32)
    col_id = (tok % GRID_W).astype(F32)
    inv = ROPE_THETA ** (-jnp.arange(0, ROPE_AXIS_DIM, 2, dtype=F32) / ROPE_AXIS_DIM)
    ang_r = row_id[:, None] * inv[None, :]
    ang_c = col_id[:, None] * inv[None, :]
    cos = jnp.concatenate([jnp.cos(ang_r)] * 2 + [jnp.cos(ang_c)] * 2, axis=-1)
    sin = jnp.concatenate([-jnp.sin(ang_r), jnp.sin(ang_r), -jnp.sin(ang_c), jnp.sin(ang_c)], axis=-1)
    return cos, sin


def _route(logits, tm, n_tiles):
    n = logits.shape[0]
    top_val, top_idx = lax.top_k(logits, TOP_K)
    gate = jax.nn.softmax(top_val, axis=-1)
    n_assign = n * TOP_K
    e_flat = top_idx.reshape(-1)
    g_flat = gate.reshape(-1)
    experts = jnp.arange(N_EXPERTS, dtype=jnp.int32)
    counts = jnp.sum((e_flat[:, None] == experts[None, :]).astype(jnp.int32), axis=0)
    tiles_per = (counts + tm - 1) // tm
    tile_end = jnp.cumsum(tiles_per)
    pstart = (tile_end - tiles_per) * tm
    start = jnp.cumsum(counts) - counts
    order = jnp.argsort(e_flat).astype(jnp.int32)
    rank = jnp.argsort(order).astype(jnp.int32)
    dest = (pstart[e_flat] + rank - start[e_flat]).reshape(n, TOP_K)
    r = jnp.arange(n_tiles * tm, dtype=jnp.int32)
    tiles = jnp.arange(n_tiles, dtype=jnp.int32)
    e_tile = jnp.minimum(jnp.sum((tile_end[None, :] <= tiles[:, None]).astype(jnp.int32), axis=1),
                         N_EXPERTS - 1)
    e_row = jnp.repeat(e_tile, tm)
    idx = r - pstart[e_row]
    real = idx < counts[e_row]
    src = order[jnp.clip(start[e_row] + idx, 0, n_assign - 1)]
    row_tok = jnp.where(real, src // TOP_K, 0)
    row_gate = jnp.where(real, g_flat[src], 0.0)
    return row_tok, row_gate, dest, tiles_per


PROJ_TM, PROJ_TN = 1024, 1024
PREP_TM = 512
ATTN_TQ = 512
GQA_HEADS_PER_STEP = 4
OUT_TM, OUT_TK = 512, 512
LN_SLAB = 64
ROUTER_TM = 512
MOE_TM, MOE_TN = 512, 512
LN_TM = 256


def kernel(x_prompt, x_sample, meta_tokens, rel_bias, w_in, lambda_q1, lambda_k1, lambda_q2,
           lambda_k2, a_subln, q_norm, k_norm, w_out, ln1_g, ln1_b, w_router, b_router,
           w1, b1, w2, b2, ln2_g, ln2_b):
    d = x_prompt.shape[-1]
    seq = x_prompt.shape[1]
    assert x_sample.shape[1] == seq and seq % GRID_W == 0
    nb_p, nb_s = x_prompt.shape[0], x_sample.shape[0]
    nb = nb_p + nb_s
    x_p, x_s = x_prompt.reshape(-1, d), x_sample.reshape(-1, d)
    n_p = x_p.shape[0]
    n = n_p + x_s.shape[0]
    x_b = jnp.concatenate([x_p.astype(BF16), x_s.astype(BF16)], axis=0)
    tq = min(ATTN_TQ, seq)

    w_in_b = w_in[0].astype(BF16)
    colscale = jnp.concatenate([jnp.full((A_WIDTH,), QUERY_SCALE, F32),
                                jnp.ones((IN_COLS - A_WIDTH,), F32)])[None]
    tn = min(PROJ_TN, IN_COLS)
    proj = _projection(x_b, w_in_b, colscale, PROJ_TM, tn)
    meta_rows = jnp.zeros((META_PAD, d), F32).at[:N_META].set(meta_tokens)
    meta_proj = _projection(meta_rows.astype(BF16), w_in_b, colscale, META_PAD, tn)

    gains = jnp.concatenate([jnp.tile(q_norm[0] * QUERY_SCALE, B_HEADS),
                             jnp.tile(k_norm[0], B_KV_HEADS)]).reshape(-1, 1, B_KV_WIDTH)
    cos, sin = _rope_tables(seq)
    bqk = _qk_prep(proj, gains, cos, sin, min(PREP_TM, seq))
    meta_bqk = _qk_prep(meta_proj, gains, jnp.ones((META_PAD, HEAD_DIM), F32),
                        jnp.zeros((META_PAD, HEAD_DIM), F32), META_PAD)

    lam = (jnp.exp(jnp.sum(lambda_q1[0].astype(F32) * lambda_k1[0].astype(F32)))
           - jnp.exp(jnp.sum(lambda_q2[0].astype(F32) * lambda_k2[0].astype(F32))) + LAMBDA_INIT)
    w_bias, far = _bias_tables(rel_bias * LOG2E, tq)
    consts = jnp.concatenate([far[0], far[1], lam[None]]).astype(F32)
    proj3 = proj.reshape(nb, seq, IN_COLS)
    a_v0, b_v0 = 2 * A_WIDTH, 3 * A_WIDTH + B_WIDTH + B_KV_WIDTH
    vt_a, meta_vt_a = _chunked_transpose(proj3[:, :, a_v0:a_v0 + A_WIDTH],
                                         meta_proj[:, a_v0:a_v0 + A_WIDTH], A_HEADS, 2 * HEAD_DIM, tq)
    vt_b, meta_vt_b = _chunked_transpose(proj3[:, :, b_v0:], meta_proj[:, b_v0:],
                                         B_KV_HEADS, HEAD_DIM, tq)
    a_out = _diff_attention(proj3, vt_a, meta_proj, meta_vt_a, w_bias, consts,
                            a_subln.astype(F32).reshape(-1, 1), tq)
    b_out = _gqa_attention(bqk.reshape(nb, seq, -1), vt_b, meta_bqk, meta_vt_b, tq,
                           GQA_HEADS_PER_STEP)

    a2, b2d, w_out_b = a_out.reshape(n, A_WIDTH), b_out.reshape(n, B_WIDTH), w_out[0].astype(BF16)
    h1, h1_b = _out_proj_ln(a2, b2d, w_out_b, x_p, x_s, ln1_g, ln1_b, OUT_TM, OUT_TK)

    logits = _router(h1, w_router[0], b_router, min(ROUTER_TM, n))
    tm = min(MOE_TM, n)
    n_tiles = (n * TOP_K + N_EXPERTS * (tm - 1)) // tm
    row_tok, row_gate, dest, tiles_per = _route(logits, tm, n_tiles)
    x_rows = h1_b[row_tok]
    dff = w2.shape[2]
    tn1, tn2 = min(MOE_TN, dff), min(MOE_TN, d)
    act = _expert_ffn1(x_rows, w1[0], b1[0].reshape(N_EXPERTS, 1, -1),
                       _work_items(tiles_per, dff // tn1, n_tiles * (dff // tn1)), tm, tn1)
    y_rows = _expert_ffn2(act, w2[0], b2[0].reshape(N_EXPERTS, 1, -1), row_gate[:, None],
                          _work_items(tiles_per, d // tn2, n_tiles * (d // tn2)), tm, tn2)
    ys = [y_rows[dest[:, k]] for k in range(TOP_K)]

    out_p = _residual_ln(h1, ys, ln2_g, ln2_b, 0, n_p, LN_TM)
    out_s = _residual_ln(h1, ys, ln2_g, ln2_b, n_p, n - n_p, LN_TM)
    return (out_p.reshape(nb_p, seq, d), out_s.reshape(nb_s, seq, d))
```

```python
import functools
import math

import jax
import jax.numpy as jnp
from jax import lax
from jax.experimental import pallas as pl
from jax.experimental.pallas import tpu as pltpu

HEAD_DIM = 128
N_META = 16
GRID_W = 64
A_HEADS = 8
A_WIDTH = A_HEADS * 2 * HEAD_DIM
B_HEADS = 16
B_KV_HEADS = 4
B_GROUP = B_HEADS // B_KV_HEADS
B_WIDTH = B_HEADS * HEAD_DIM
B_KV_WIDTH = B_KV_HEADS * HEAD_DIM
IN_COLS = 3 * A_WIDTH + B_WIDTH + 2 * B_KV_WIDTH
ROPE_THETA = 10000.0
ROPE_AXIS_DIM = HEAD_DIM // 2
REL_BUCKETS = 32
REL_MAX_DIST = 128
N_EXPERTS = 32
TOP_K = 4
SWIGLU_ALPHA = 1.702
SWIGLU_LIMIT = 7.0
DEPTH = 1
DEEPNORM_ALPHA = (2 * DEPTH) ** 0.25
LN_EPS = 1e-5
RMS_EPS = 1e-6
LAMBDA_INIT = 0.8 - 0.6 * math.exp(-0.3 * 0)
LOG2E = math.log2(math.e)
QUERY_SCALE = HEAD_DIM ** -0.5 * LOG2E

LANES = 128
META_PAD = 128
MASK_VALUE = -1e30
VMEM_LIMIT = 56 * 1024 * 1024

BF16 = jnp.bfloat16
F32 = jnp.float32


def _cparams(sem):
    return pltpu.CompilerParams(dimension_semantics=sem, vmem_limit_bytes=VMEM_LIMIT)


def _proj_kernel(a_ref, b_ref, s_ref, o_ref):
    acc = jnp.dot(a_ref[...], b_ref[...], preferred_element_type=F32)
    o_ref[...] = (acc * s_ref[...]).astype(o_ref.dtype)


def _projection(a, b, colscale, tm, tn):
    m, k = a.shape
    n = b.shape[1]
    tm = min(tm, m)
    return pl.pallas_call(
        _proj_kernel,
        grid=(m // tm, n // tn),
        in_specs=[
            pl.BlockSpec((tm, k), lambda i, j: (i, 0)),
            pl.BlockSpec((k, tn), lambda i, j: (0, j)),
            pl.BlockSpec((1, tn), lambda i, j: (0, j)),
        ],
        out_specs=pl.BlockSpec((tm, tn), lambda i, j: (i, j)),
        out_shape=jax.ShapeDtypeStruct((m, n), BF16),
        compiler_params=_cparams(("parallel", "parallel")),
        name="in_proj",
    )(a, b, colscale)


def _qk_prep_kernel(x_ref, g_ref, c_ref, s_ref, o_ref):
    cos = c_ref[...]
    sin = s_ref[...]
    lane = lax.broadcasted_iota(jnp.int32, cos.shape, 1)
    first_half = (lane % (ROPE_AXIS_DIM)) < (ROPE_AXIS_DIM // 2)
    for hh in range(x_ref.shape[1] // HEAD_DIM):
        sl = slice(hh * HEAD_DIM, (hh + 1) * HEAD_DIM)
        x = x_ref[:, sl].astype(F32)
        ms = jnp.mean(x * x, axis=-1, keepdims=True)
        y = x * lax.rsqrt(ms + RMS_EPS) * g_ref[:, sl]
        partner = jnp.where(first_half,
                            pltpu.roll(y, HEAD_DIM - ROPE_AXIS_DIM // 2, 1),
                            pltpu.roll(y, ROPE_AXIS_DIM // 2, 1))
        o_ref[:, sl] = (y * cos + partner * sin).astype(o_ref.dtype)


def _qk_prep(proj, gains, cos, sin, tm):
    m = proj.shape[0]
    tm = min(tm, m)
    w = B_KV_WIDTH
    first = (3 * A_WIDTH) // w
    nblk = (B_WIDTH + B_KV_WIDTH) // w
    ntab = cos.shape[0] // tm
    return pl.pallas_call(
        _qk_prep_kernel,
        grid=(m // tm, nblk),
        in_specs=[
            pl.BlockSpec((tm, w), lambda i, j: (i, first + j)),
            pl.BlockSpec((None, 1, w), lambda i, j: (j, 0, 0)),
            pl.BlockSpec((tm, HEAD_DIM), lambda i, j: (i % ntab, 0)),
            pl.BlockSpec((tm, HEAD_DIM), lambda i, j: (i % ntab, 0)),
        ],
        out_specs=pl.BlockSpec((tm, w), lambda i, j: (i, j)),
        out_shape=jax.ShapeDtypeStruct((m, B_WIDTH + B_KV_WIDTH), BF16),
        compiler_params=_cparams(("parallel", "parallel")),
        name="qk_prep",
    )(proj, gains, cos, sin)


def _scores_t(k, q):
    return lax.dot_general(k, q, (((1,), (1,)), ((), ())), preferred_element_type=F32)


def _softmax_init(s, v_t, m_ref, l_ref, acc_ref):
    m = jnp.max(s, axis=0, keepdims=True)
    p = jnp.exp2(s - m)
    m_ref[...] = m
    l_ref[...] = jnp.sum(p, axis=0, keepdims=True)
    acc_ref[...] = jnp.dot(v_t, p.astype(v_t.dtype), preferred_element_type=F32)


def _softmax_update(s, shift, v_t, m_ref, l_ref, acc_ref):
    m_prev = m_ref[...]
    m_new = jnp.maximum(m_prev, jnp.max(s, axis=0, keepdims=True) + shift)
    p = jnp.exp2(s - (m_new - shift))
    alpha = jnp.exp2(m_prev - m_new)
    l_ref[...] = alpha * l_ref[...] + jnp.sum(p, axis=0, keepdims=True)
    acc_ref[...] = alpha * acc_ref[...] + jnp.dot(v_t, p.astype(v_t.dtype),
                                                  preferred_element_type=F32)
    m_ref[...] = m_new


def _diff_attn_kernel(c_ref, q1_ref, q2_ref, k1_ref, k2_ref, vt_ref, mk1_ref, mk2_ref, mvt_ref,
                      w_ref, g_ref, o_ref, band_ref, mbias_ref, m_ref, l_ref, acc_ref,
                      sa_ref, sb_ref, *, tk):
    h = pl.program_id(1)
    i = pl.program_id(2)
    tq = q1_ref.shape[0]
    nk = k1_ref.shape[0] // tk
    bias_left = c_ref[h]
    bias_right = c_ref[A_HEADS + h]
    lam = c_ref[2 * A_HEADS]
    q1 = q1_ref[...]
    q2 = q2_ref[...]
    stats = (m_ref, l_ref, acc_ref)

    def scores_into(dst, j):
        rows = pl.ds(pl.multiple_of(j * tk, tk), tk)
        dst[:, :tq] = _scores_t(k1_ref[rows, :], q1)
        dst[:, tq:] = _scores_t(k2_ref[rows, :], q2)

    scores_into(sa_ref, 0)

    @pl.when(i == 0)
    def _():
        width = w_ref.shape[1]
        rolled = pltpu.roll(jnp.broadcast_to(w_ref[...], (tk, width)), 0, 1,
                            stride=1, stride_axis=0)
        band_ref[0] = jnp.full((tk, tq), bias_left, F32)
        for d in range(3):
            band_ref[1 + d] = rolled[:, (3 - d) * tq:(4 - d) * tq]
        band_ref[4] = jnp.full((tk, tq), bias_right, F32)
        mrolled = pltpu.roll(jnp.broadcast_to(w_ref[...], (META_PAD, width)), width - N_META, 1,
                             stride=1, stride_axis=0)
        mbias_ref[...] = mrolled[:, 2 * tq:3 * tq]

    def both(x):
        return jnp.concatenate([x, x], axis=1)

    meta_ok = lax.broadcasted_iota(jnp.int32, (META_PAD, 2 * tq), 0) < N_META
    meta_bias = jnp.where(i == 0, mbias_ref[...], bias_left)
    s = jnp.concatenate([_scores_t(mk1_ref[...], q1), _scores_t(mk2_ref[...], q2)], axis=1)
    _softmax_init(jnp.where(meta_ok, s + both(meta_bias), MASK_VALUE), mvt_ref[...], *stats)

    def consume(src, j):
        bias = band_ref[jnp.clip(j - i, -2, 2) + 2]
        _softmax_update(src[...] + both(bias), 0.0, vt_ref[j], *stats)

    def body(jj, carry):
        j = 2 * jj
        scores_into(sb_ref, j + 1)
        consume(sa_ref, j)
        scores_into(sa_ref, jnp.minimum(j + 2, nk - 1))
        consume(sb_ref, j + 1)
        return carry

    lax.fori_loop(0, nk // 2, body, 0)

    o = acc_ref[...] * (1.0 / l_ref[...])
    o = o[:, :tq] - lam * o[:, tq:]
    ms = jnp.mean(o * o, axis=0, keepdims=True)
    o = o * lax.rsqrt(ms + RMS_EPS) * g_ref[...] * (1.0 - LAMBDA_INIT)
    o_ref[...] = o.T.astype(o_ref.dtype)


def _diff_attention(proj3, vt, meta_proj, meta_vt, w_bias, consts, subln, tq):
    nb, seq, _ = proj3.shape
    dv = 2 * HEAD_DIM
    nk = seq // tq
    kq2 = A_HEADS
    kk1 = A_WIDTH // HEAD_DIM
    kk2 = kk1 + A_HEADS
    assert nk % 2 == 0
    stat = [pltpu.VMEM((1, 2 * tq), F32), pltpu.VMEM((1, 2 * tq), F32), pltpu.VMEM((dv, 2 * tq), F32),
            pltpu.VMEM((tq, 2 * tq), F32), pltpu.VMEM((tq, 2 * tq), F32)]
    return pl.pallas_call(
        functools.partial(_diff_attn_kernel, tk=tq),
        grid=(nb, A_HEADS, nk),
        in_specs=[
            pl.BlockSpec(memory_space=pltpu.SMEM),
            pl.BlockSpec((None, tq, HEAD_DIM), lambda b, h, i: (b, i, h)),
            pl.BlockSpec((None, tq, HEAD_DIM), lambda b, h, i: (b, i, kq2 + h)),
            pl.BlockSpec((None, seq, HEAD_DIM), lambda b, h, i: (b, 0, kk1 + h)),
            pl.BlockSpec((None, seq, HEAD_DIM), lambda b, h, i: (b, 0, kk2 + h)),
            pl.BlockSpec((None, None, nk, dv, tq), lambda b, h, i: (b, h, 0, 0, 0)),
            pl.BlockSpec((META_PAD, HEAD_DIM), lambda b, h, i: (0, kk1 + h)),
            pl.BlockSpec((META_PAD, HEAD_DIM), lambda b, h, i: (0, kk2 + h)),
            pl.BlockSpec((None, dv, META_PAD), lambda b, h, i: (h, 0, 0)),
            pl.BlockSpec((None, 1, 4 * tq), lambda b, h, i: (h, 0, 0)),
            pl.BlockSpec((dv, 1), lambda b, h, i: (0, 0)),
        ],
        out_specs=pl.BlockSpec((None, tq, dv), lambda b, h, i: (b, i, h)),
        out_shape=jax.ShapeDtypeStruct((nb, seq, A_WIDTH), BF16),
        scratch_shapes=[pltpu.VMEM((5, tq, tq), F32), pltpu.VMEM((META_PAD, tq), F32)] + stat,
        compiler_params=_cparams(("parallel", "parallel", "arbitrary")),
        name="diff_attn",
    )(consts, proj3, proj3, proj3, proj3, vt, meta_proj, meta_proj, meta_vt, w_bias, subln)


def _gqa_kernel(q_ref, k_ref, vt_ref, mk_ref, mvt_ref, o_ref, m_ref, l_ref, acc_ref,
                sa_ref, sb_ref, *, tk, heads):
    nk = k_ref.shape[0] // tk
    tq = q_ref.shape[0]
    q = jnp.concatenate([q_ref[:, g * HEAD_DIM:(g + 1) * HEAD_DIM] for g in range(heads)], axis=0)
    stats = (m_ref, l_ref, acc_ref)

    def scores_into(dst, j):
        dst[...] = _scores_t(k_ref[pl.ds(pl.multiple_of(j * tk, tk), tk), :], q)

    scores_into(sa_ref, 0)
    meta_ok = lax.broadcasted_iota(jnp.int32, (META_PAD, heads * tq), 0) < N_META
    s = jnp.where(meta_ok, _scores_t(mk_ref[...], q), MASK_VALUE)
    _softmax_init(s, mvt_ref[...], *stats)

    def body(jj, carry):
        j = 2 * jj
        scores_into(sb_ref, j + 1)
        _softmax_update(sa_ref[...], 0.0, vt_ref[j], *stats)
        scores_into(sa_ref, jnp.minimum(j + 2, nk - 1))
        _softmax_update(sb_ref[...], 0.0, vt_ref[j + 1], *stats)
        return carry

    lax.fori_loop(0, nk // 2, body, 0)
    o = acc_ref[...] * (1.0 / l_ref[...])
    for g in range(heads):
        o_ref[:, g * HEAD_DIM:(g + 1) * HEAD_DIM] = o[:, g * tq:(g + 1) * tq].T.astype(o_ref.dtype)


def _gqa_attention(bqk3, vt, meta_bqk, meta_vt, tq, heads):
    nb, seq, _ = bqk3.shape
    nk = seq // tq
    assert nk % 2 == 0
    kk = B_HEADS
    per_group = B_GROUP // heads
    wide = heads * tq
    stat = [pltpu.VMEM((1, wide), F32), pltpu.VMEM((1, wide), F32), pltpu.VMEM((HEAD_DIM, wide), F32),
            pltpu.VMEM((tq, wide), F32), pltpu.VMEM((tq, wide), F32)]
    return pl.pallas_call(
        functools.partial(_gqa_kernel, tk=tq, heads=heads),
        grid=(nb, B_HEADS // heads, nk),
        in_specs=[
            pl.BlockSpec((None, tq, heads * HEAD_DIM), lambda b, h, i: (b, i, h)),
            pl.BlockSpec((None, seq, HEAD_DIM), lambda b, h, i: (b, 0, kk + h // per_group)),
            pl.BlockSpec((None, None, nk, HEAD_DIM, tq),
                         lambda b, h, i: (b, h // per_group, 0, 0, 0)),
            pl.BlockSpec((META_PAD, HEAD_DIM), lambda b, h, i: (0, kk + h // per_group)),
            pl.BlockSpec((None, HEAD_DIM, META_PAD), lambda b, h, i: (h // per_group, 0, 0)),
        ],
        out_specs=pl.BlockSpec((None, tq, heads * HEAD_DIM), lambda b, h, i: (b, i, h)),
        out_shape=jax.ShapeDtypeStruct((nb, seq, B_WIDTH), BF16),
        scratch_shapes=stat,
        compiler_params=_cparams(("parallel", "parallel", "parallel")),
        name="gqa_attn",
    )(bqk3, bqk3, vt, meta_bqk, meta_vt)


def _chunked_transpose(v3, meta_v, heads, dv, tk):
    nb, seq, _ = v3.shape
    vt = v3.reshape(nb, seq // tk, tk, heads, dv).transpose(0, 3, 1, 4, 2)
    meta_vt = meta_v.reshape(META_PAD, heads, dv).transpose(1, 2, 0)
    return vt, meta_vt


def _layer_norm(y, g, b):
    mu = jnp.mean(y, axis=-1, keepdims=True)
    yc = y - mu
    var = jnp.mean(yc * yc, axis=-1, keepdims=True)
    return yc * lax.rsqrt(var + LN_EPS) * g + b


def _out_proj_ln_kernel(a_ref, b_ref, w_ref, xp_hbm, xs_hbm, g_ref, beta_ref, h_ref, hb_ref,
                        acc_ref, x_ref, sem, *, ka, tiles_p):
    i = pl.program_id(0)
    k = pl.program_id(1)
    tm = acc_ref.shape[0]

    def x_copy(src, tile):
        rows = pl.ds(pl.multiple_of(tile * tm, tm), tm)
        return pltpu.make_async_copy(src.at[rows, :], x_ref, sem.at[0])

    def for_x_source(fn):
        @pl.when(i < tiles_p)
        def _():
            fn(x_copy(xp_hbm, i))

        @pl.when(i >= tiles_p)
        def _():
            fn(x_copy(xs_hbm, i - tiles_p))

    @pl.when(k == 0)
    def _():
        acc_ref[...] = jnp.zeros_like(acc_ref)
        for_x_source(lambda c: c.start())

    @pl.when(k < ka)
    def _():
        acc_ref[...] += jnp.dot(a_ref[...], w_ref[...], preferred_element_type=F32)

    @pl.when(k >= ka)
    def _():
        acc_ref[...] += jnp.dot(b_ref[...], w_ref[...], preferred_element_type=F32)

    @pl.when(k == pl.num_programs(1) - 1)
    def _():
        for_x_source(lambda c: c.wait())

        def body(c, carry):
            sl = pl.ds(pl.multiple_of(c * LN_SLAB, LN_SLAB), LN_SLAB)
            h = _layer_norm(DEEPNORM_ALPHA * x_ref[sl, :] + acc_ref[sl, :], g_ref[...], beta_ref[...])
            h_ref[sl, :] = h
            hb_ref[sl, :] = h.astype(hb_ref.dtype)
            return carry
        lax.fori_loop(0, acc_ref.shape[0] // LN_SLAB, body, 0)


def _out_proj_ln(a_out, b_out, w_out, x_p, x_s, g, beta, tm, tk):
    n_p, d = x_p.shape
    n = n_p + x_s.shape[0]
    tm = math.gcd(math.gcd(tm, n_p), x_s.shape[0])
    ka = a_out.shape[1] // tk
    kb = b_out.shape[1] // tk
    out_block = pl.BlockSpec((tm, d), lambda i, k: (i, 0), pipeline_mode=pl.Buffered(1))
    return pl.pallas_call(
        functools.partial(_out_proj_ln_kernel, ka=ka, tiles_p=n_p // tm),
        grid=(n // tm, ka + kb),
        in_specs=[
            pl.BlockSpec((tm, tk), lambda i, k: (i, jnp.minimum(k, ka - 1))),
            pl.BlockSpec((tm, tk), lambda i, k: (i, jnp.maximum(k - ka, 0))),
            pl.BlockSpec((tk, d), lambda i, k: (k, 0)),
            pl.BlockSpec(memory_space=pl.ANY),
            pl.BlockSpec(memory_space=pl.ANY),
            pl.BlockSpec((1, d), lambda i, k: (0, 0)),
            pl.BlockSpec((1, d), lambda i, k: (0, 0)),
        ],
        out_specs=[out_block, out_block],
        out_shape=[jax.ShapeDtypeStruct((n, d), F32), jax.ShapeDtypeStruct((n, d), BF16)],
        scratch_shapes=[pltpu.VMEM((tm, d), F32), pltpu.VMEM((tm, d), F32),
                        pltpu.SemaphoreType.DMA((1,))],
        compiler_params=_cparams(("parallel", "arbitrary")),
        name="out_proj_ln1",
    )(a_out, b_out, w_out, x_p, x_s, g, beta)


def _router_kernel(h_ref, w_ref, b_ref, o_ref):
    o_ref[...] = jnp.dot(h_ref[...], w_ref[...], precision=lax.Precision.HIGHEST,
                         preferred_element_type=F32) + b_ref[...]


def _router(h, w, b, tm):
    m, d = h.shape
    e = w.shape[1]
    return pl.pallas_call(
        _router_kernel,
        grid=(m // tm,),
        in_specs=[
            pl.BlockSpec((tm, d), lambda i: (i, 0)),
            pl.BlockSpec((d, e), lambda i: (0, 0)),
            pl.BlockSpec((1, e), lambda i: (0, 0)),
        ],
        out_specs=pl.BlockSpec((tm, e), lambda i: (i, 0)),
        out_shape=jax.ShapeDtypeStruct((m, e), F32),
        compiler_params=_cparams(("parallel",)),
        name="router",
    )(h, w, b)


class _Items:
    EXPERT, COL, ROW, IN_ROW, GROUP, FIRST, NEXT_EXPERT, NEXT_COL, HAS_NEXT, COUNT = range(10)
    N = 10


def _stream_weights(s, w_hbm, col_offsets, wbuf, sem, tn):
    g = pl.program_id(0)

    def copies(e, j, slot):
        return [pltpu.make_async_copy(
            w_hbm.at[e, :, pl.ds(pl.multiple_of((off + j) * tn, tn), tn)],
            wbuf.at[slot, c], sem.at[slot, c]) for c, off in enumerate(col_offsets)]

    slot = s[_Items.GROUP][g] % 2

    @pl.when(g == 0)
    def _():
        for c in copies(s[_Items.EXPERT][0], s[_Items.COL][0], 0):
            c.start()

    @pl.when(s[_Items.FIRST][g] == 1)
    def _():
        for c in copies(s[_Items.EXPERT][g], s[_Items.COL][g], slot):
            c.wait()

        @pl.when(s[_Items.HAS_NEXT][g] == 1)
        def _():
            for c in copies(s[_Items.NEXT_EXPERT][g], s[_Items.NEXT_COL][g], 1 - slot):
                c.start()

    return slot


def _ffn1_kernel(*refs, ncol, tn):
    s = refs[:_Items.N]
    x_ref, w_hbm, bg_ref, bl_ref, o_ref, wbuf, sem = refs[_Items.N:]

    @pl.when(pl.program_id(0) < s[_Items.COUNT][0])
    def _():
        slot = _stream_weights(s, w_hbm, (0, ncol), wbuf, sem, tn)
        x = x_ref[...]
        hg = jnp.dot(x, wbuf[slot, 0].astype(BF16), preferred_element_type=F32) + bg_ref[...]
        hl = jnp.dot(x, wbuf[slot, 1].astype(BF16), preferred_element_type=F32) + bl_ref[...]
        hg = jnp.minimum(hg, SWIGLU_LIMIT)
        hl = jnp.clip(hl, -SWIGLU_LIMIT, SWIGLU_LIMIT)
        act = hg * (1.0 / (1.0 + jnp.exp(-SWIGLU_ALPHA * hg))) * (hl + 1.0)
        o_ref[...] = act.astype(o_ref.dtype)

    @pl.when(pl.program_id(0) >= s[_Items.COUNT][0])
    def _():
        o_ref[...] = jnp.zeros_like(o_ref)


def _ffn2_kernel(*refs, tn):
    s = refs[:_Items.N]
    a_ref, w_hbm, b_ref, g_ref, o_ref, wbuf, sem = refs[_Items.N:]

    @pl.when(pl.program_id(0) < s[_Items.COUNT][0])
    def _():
        slot = _stream_weights(s, w_hbm, (0,), wbuf, sem, tn)
        y = jnp.dot(a_ref[...], wbuf[slot, 0].astype(BF16), preferred_element_type=F32) + b_ref[...]
        o_ref[...] = (y * g_ref[...]).astype(o_ref.dtype)

    @pl.when(pl.program_id(0) >= s[_Items.COUNT][0])
    def _():
        o_ref[...] = jnp.zeros_like(o_ref)


def _work_items(tiles_per, ncol, n_items_max):
    i32 = jnp.int32
    experts = jnp.arange(N_EXPERTS, dtype=i32)
    tile_end = jnp.cumsum(tiles_per)
    tile_start = tile_end - tiles_per
    items_per = tiles_per * ncol
    item_end = jnp.cumsum(items_per)
    n_items = item_end[-1:]
    g_all = jnp.arange(n_items_max, dtype=i32)
    fill = g_all - n_items[0]
    g = jnp.minimum(g_all, n_items[0] - 1)
    e = jnp.minimum(jnp.sum((item_end[None, :] <= g[:, None]).astype(i32), axis=1), N_EXPERTS - 1)
    local = g - (item_end - items_per)[e]
    per = jnp.maximum(tiles_per[e], 1)
    j = local // per
    t = local % per
    used = tiles_per > 0
    groups_before = (jnp.cumsum(used.astype(i32)) - used.astype(i32)) * ncol
    later = jnp.where(used[None, :] & (experts[None, :] > experts[:, None]), experts[None, :], N_EXPERTS)
    next_used = jnp.min(later, axis=1)
    last_col = j == ncol - 1
    next_e = jnp.where(last_col, next_used[e], e)
    has_next = next_e < N_EXPERTS
    row = tile_start[e] + t
    return tuple(a.astype(i32) for a in (
        e, jnp.where(fill >= 0, fill % ncol, j), jnp.where(fill >= 0, tile_end[-1] + fill // ncol, row),
        row, groups_before[e] + j, t == 0,
        jnp.minimum(next_e, N_EXPERTS - 1), jnp.where(last_col, 0, j + 1), has_next, n_items))


def _expert_ffn1(x_rows, w1, b1, items, tm, tn):
    n_rows, d = x_rows.shape
    dff = w1.shape[2] // 2
    ncol = dff // tn
    E, C, R = _Items.EXPERT, _Items.COL, _Items.ROW
    grid_spec = pltpu.PrefetchScalarGridSpec(
        num_scalar_prefetch=_Items.N,
        grid=(items[0].shape[0],),
        in_specs=[
            pl.BlockSpec((tm, d), lambda g, *s: (s[_Items.IN_ROW][g], 0)),
            pl.BlockSpec(memory_space=pl.ANY),
            pl.BlockSpec((None, 1, tn), lambda g, *s: (s[E][g], 0, s[C][g])),
            pl.BlockSpec((None, 1, tn), lambda g, *s: (s[E][g], 0, ncol + s[C][g])),
        ],
        out_specs=pl.BlockSpec((tm, tn), lambda g, *s: (s[R][g], s[C][g])),
        scratch_shapes=[pltpu.VMEM((2, 2, d, tn), F32), pltpu.SemaphoreType.DMA((2, 2))],
    )
    return pl.pallas_call(
        functools.partial(_ffn1_kernel, ncol=ncol, tn=tn),
        grid_spec=grid_spec,
        out_shape=jax.ShapeDtypeStruct((n_rows, dff), BF16),
        compiler_params=_cparams(("arbitrary",)),
        name="moe_ffn1",
    )(*items, x_rows, w1, b1, b1)


def _expert_ffn2(act, w2, b2, row_gate, items, tm, tn):
    n_rows, dff = act.shape
    d = w2.shape[2]
    E, C, R = _Items.EXPERT, _Items.COL, _Items.ROW
    grid_spec = pltpu.PrefetchScalarGridSpec(
        num_scalar_prefetch=_Items.N,
        grid=(items[0].shape[0],),
        in_specs=[
            pl.BlockSpec((tm, dff), lambda g, *s: (s[_Items.IN_ROW][g], 0)),
            pl.BlockSpec(memory_space=pl.ANY),
            pl.BlockSpec((None, 1, tn), lambda g, *s: (s[E][g], 0, s[C][g])),
            pl.BlockSpec((tm, 1), lambda g, *s: (s[_Items.IN_ROW][g], 0)),
        ],
        out_specs=pl.BlockSpec((tm, tn), lambda g, *s: (s[R][g], s[C][g])),
        scratch_shapes=[pltpu.VMEM((2, 1, dff, tn), F32), pltpu.SemaphoreType.DMA((2, 1))],
    )
    return pl.pallas_call(
        functools.partial(_ffn2_kernel, tn=tn),
        grid_spec=grid_spec,
        out_shape=jax.ShapeDtypeStruct((n_rows, d), BF16),
        compiler_params=_cparams(("arbitrary",)),
        name="moe_ffn2",
    )(*items, act, w2, b2, row_gate)


def _residual_ln_kernel(h_ref, *refs):
    y_refs, (g_ref, b_ref, o_ref) = refs[:TOP_K], refs[TOP_K:]
    y = y_refs[0][...].astype(F32)
    for y_ref in y_refs[1:]:
        y = y + y_ref[...].astype(F32)
    o_ref[...] = _layer_norm(DEEPNORM_ALPHA * h_ref[...] + y, g_ref[...], b_ref[...])


def _residual_ln(h, ys, g, b, row0, m, tm):
    d = h.shape[1]
    tm = min(tm, m)
    off = row0 // tm
    tile = pl.BlockSpec((tm, d), lambda i: (i + off, 0))
    vec = pl.BlockSpec((1, d), lambda i: (0, 0))
    return pl.pallas_call(
        _residual_ln_kernel,
        grid=(m // tm,),
        in_specs=[tile] * (1 + TOP_K) + [vec, vec],
        out_specs=pl.BlockSpec((tm, d), lambda i: (i, 0)),
        out_shape=jax.ShapeDtypeStruct((m, d), F32),
        compiler_params=_cparams(("parallel",)),
        name="residual_ln2",
    )(h, *ys, g, b)


def _rel_bucket(rel):
    nb = REL_BUCKETS // 2
    max_exact = nb // 2
    ret = jnp.where(rel > 0, nb, 0)
    n = jnp.abs(rel)
    nf = jnp.maximum(n, 1).astype(F32)
    large = max_exact + (jnp.log(nf / max_exact) / math.log(REL_MAX_DIST / max_exact)
                         * (nb - max_exact)).astype(jnp.int32)
    large = jnp.minimum(large, nb - 1)
    return ret + jnp.where(n < max_exact, n, large)


def _bias_tables(rel_bias, tq):
    assert tq >= REL_MAX_DIST
    rel = 2 * tq - jnp.arange(4 * tq, dtype=jnp.int32)
    w = rel_bias[_rel_bucket(rel)].astype(F32).T[:, None, :]
    far = rel_bias[_rel_bucket(jnp.array([-(tq + 1), tq + 1], jnp.int32))].astype(F32)
    return w, far


def _rope_tables(seq):
    tok = jnp.arange(seq, dtype=jnp.int32)
    row_id = (tok // GRID_W).astype(F32)
    col_id = (tok % GRID_W).astype(F32)
    inv = ROPE_THETA ** (-jnp.arange(0, ROPE_AXIS_DIM, 2, dtype=F32) / ROPE_AXIS_DIM)
    ang_r = row_id[:, None] * inv[None, :]
    ang_c = col_id[:, None] * inv[None, :]
    cos = jnp.concatenate([jnp.cos(ang_r)] * 2 + [jnp.cos(ang_c)] * 2, axis=-1)
    sin = jnp.concatenate([-jnp.sin(ang_r), jnp.sin(ang_r), -jnp.sin(ang_c), jnp.sin(ang_c)], axis=-1)
    return cos, sin


def _route(logits, tm, n_tiles):
    n = logits.shape[0]
    top_val, top_idx = lax.top_k(logits, TOP_K)
    gate = jax.nn.softmax(top_val, axis=-1)
    n_assign = n * TOP_K
    e_flat = top_idx.reshape(-1)
    g_flat = gate.reshape(-1)
    experts = jnp.arange(N_EXPERTS, dtype=jnp.int32)
    counts = jnp.sum((experts[:, None] == e_flat[None, :]).astype(jnp.int32), axis=1)
    tiles_per = (counts + tm - 1) // tm
    tile_end = jnp.cumsum(tiles_per)
    pstart = (tile_end - tiles_per) * tm
    start = jnp.cumsum(counts) - counts
    order = jnp.argsort(e_flat).astype(jnp.int32)
    rank = jnp.argsort(order).astype(jnp.int32)
    dest = (pstart[e_flat] + rank - start[e_flat]).reshape(n, TOP_K)
    r = jnp.arange(n_tiles * tm, dtype=jnp.int32)
    tiles = jnp.arange(n_tiles, dtype=jnp.int32)
    e_tile = jnp.minimum(jnp.sum((tile_end[None, :] <= tiles[:, None]).astype(jnp.int32), axis=1),
                         N_EXPERTS - 1)
    e_row = jnp.repeat(e_tile, tm)
    idx = r - pstart[e_row]
    real = idx < counts[e_row]
    src = order[jnp.clip(start[e_row] + idx, 0, n_assign - 1)]
    row_tok = jnp.where(real, src // TOP_K, 0)
    row_gate = jnp.where(real, g_flat[src], 0.0)
    return row_tok, row_gate, dest, tiles_per


PROJ_TM, PROJ_TN = 1024, 1024
PREP_TM = 512
ATTN_TQ = 512
GQA_HEADS_PER_STEP = 4
OUT_TM, OUT_TK = 512, 512
LN_SLAB = 64
ROUTER_TM = 512
MOE_TM, MOE_TN = 512, 512
LN_TM = 256


def kernel(x_prompt, x_sample, meta_tokens, rel_bias, w_in, lambda_q1, lambda_k1, lambda_q2,
           lambda_k2, a_subln, q_norm, k_norm, w_out, ln1_g, ln1_b, w_router, b_router,
           w1, b1, w2, b2, ln2_g, ln2_b):
    d = x_prompt.shape[-1]
    seq = x_prompt.shape[1]
    assert x_sample.shape[1] == seq and seq % GRID_W == 0
    nb_p, nb_s = x_prompt.shape[0], x_sample.shape[0]
    nb = nb_p + nb_s
    x_p, x_s = x_prompt.reshape(-1, d), x_sample.reshape(-1, d)
    n_p = x_p.shape[0]
    n = n_p + x_s.shape[0]
    x_b = jnp.concatenate([x_p.astype(BF16), x_s.astype(BF16)], axis=0)
    tq = min(ATTN_TQ, seq)

    w_in_b = w_in[0].astype(BF16)
    colscale = jnp.concatenate([jnp.full((A_WIDTH,), QUERY_SCALE, F32),
                                jnp.ones((IN_COLS - A_WIDTH,), F32)])[None]
    tn = min(PROJ_TN, IN_COLS)
    proj = _projection(x_b, w_in_b, colscale, PROJ_TM, tn)
    meta_rows = jnp.zeros((META_PAD, d), F32).at[:N_META].set(meta_tokens)
    meta_proj = _projection(meta_rows.astype(BF16), w_in_b, colscale, META_PAD, tn)

    gains = jnp.concatenate([jnp.tile(q_norm[0] * QUERY_SCALE, B_HEADS),
                             jnp.tile(k_norm[0], B_KV_HEADS)]).reshape(-1, 1, B_KV_WIDTH)
    cos, sin = _rope_tables(seq)
    bqk = _qk_prep(proj, gains, cos, sin, min(PREP_TM, seq))
    meta_bqk = _qk_prep(meta_proj, gains, jnp.ones((META_PAD, HEAD_DIM), F32),
                        jnp.zeros((META_PAD, HEAD_DIM), F32), META_PAD)

    lam = (jnp.exp(jnp.sum(lambda_q1[0].astype(F32) * lambda_k1[0].astype(F32)))
           - jnp.exp(jnp.sum(lambda_q2[0].astype(F32) * lambda_k2[0].astype(F32))) + LAMBDA_INIT)
    w_bias, far = _bias_tables(rel_bias * LOG2E, tq)
    consts = jnp.concatenate([far[0], far[1], lam[None]]).astype(F32)
    proj3 = proj.reshape(nb, seq, IN_COLS)
    a_v0, b_v0 = 2 * A_WIDTH, 3 * A_WIDTH + B_WIDTH + B_KV_WIDTH
    vt_a, meta_vt_a = _chunked_transpose(proj3[:, :, a_v0:a_v0 + A_WIDTH],
                                         meta_proj[:, a_v0:a_v0 + A_WIDTH], A_HEADS, 2 * HEAD_DIM, tq)
    vt_b, meta_vt_b = _chunked_transpose(proj3[:, :, b_v0:], meta_proj[:, b_v0:],
                                         B_KV_HEADS, HEAD_DIM, tq)
    a_out = _diff_attention(proj3, vt_a, meta_proj, meta_vt_a, w_bias, consts,
                            a_subln.astype(F32).reshape(-1, 1), tq)
    b_out = _gqa_attention(bqk.reshape(nb, seq, -1), vt_b, meta_bqk, meta_vt_b, tq,
                           GQA_HEADS_PER_STEP)

    a2, b2d, w_out_b = a_out.reshape(n, A_WIDTH), b_out.reshape(n, B_WIDTH), w_out[0].astype(BF16)
    h1, h1_b = _out_proj_ln(a2, b2d, w_out_b, x_p, x_s, ln1_g, ln1_b, OUT_TM, OUT_TK)

    logits = _router(h1, w_router[0], b_router, min(ROUTER_TM, n))
    tm = min(MOE_TM, n)
    n_tiles = (n * TOP_K + N_EXPERTS * (tm - 1)) // tm
    row_tok, row_gate, dest, tiles_per = _route(logits, tm, n_tiles)
    x_rows = h1_b[row_tok]
    dff = w2.shape[2]
    tn1, tn2 = min(MOE_TN, dff), min(MOE_TN, d)
    act = _expert_ffn1(x_rows, w1[0], b1[0].reshape(N_EXPERTS, 1, -1),
                       _work_items(tiles_per, dff // tn1, n_tiles * (dff // tn1)), tm, tn1)
    y_rows = _expert_ffn2(act, w2[0], b2[0].reshape(N_EXPERTS, 1, -1), row_gate[:, None],
                          _work_items(tiles_per, d // tn2, n_tiles * (d // tn2)), tm, tn2)
    ys = [y_rows[dest[:, k]] for k in range(TOP_K)]

    out_p = _residual_ln(h1, ys, ln2_g, ln2_b, 0, n_p, LN_TM)
    out_s = _residual_ln(h1, ys, ln2_g, ln2_b, n_p, n - n_p, LN_TM)
    return (out_p.reshape(nb_p, seq, d), out_s.reshape(nb_s, seq, d))
```

```python
import functools
import math

import jax
import jax.numpy as jnp
from jax import lax
from jax.experimental import pallas as pl
from jax.experimental.pallas import tpu as pltpu

HEAD_DIM = 128
N_META = 16
GRID_W = 64
A_HEADS = 8
A_WIDTH = A_HEADS * 2 * HEAD_DIM
B_HEADS = 16
B_KV_HEADS = 4
B_GROUP = B_HEADS // B_KV_HEADS
B_WIDTH = B_HEADS * HEAD_DIM
B_KV_WIDTH = B_KV_HEADS * HEAD_DIM
IN_COLS = 3 * A_WIDTH + B_WIDTH + 2 * B_KV_WIDTH
ROPE_THETA = 10000.0
ROPE_AXIS_DIM = HEAD_DIM // 2
REL_BUCKETS = 32
REL_MAX_DIST = 128
N_EXPERTS = 32
TOP_K = 4
SWIGLU_ALPHA = 1.702
SWIGLU_LIMIT = 7.0
DEPTH = 1
DEEPNORM_ALPHA = (2 * DEPTH) ** 0.25
LN_EPS = 1e-5
RMS_EPS = 1e-6
LAMBDA_INIT = 0.8 - 0.6 * math.exp(-0.3 * 0)
LOG2E = math.log2(math.e)
QUERY_SCALE = HEAD_DIM ** -0.5 * LOG2E

LANES = 128
META_PAD = 128
MASK_VALUE = -1e30
VMEM_LIMIT = 56 * 1024 * 1024

BF16 = jnp.bfloat16
F32 = jnp.float32


def _cparams(sem):
    return pltpu.CompilerParams(dimension_semantics=sem, vmem_limit_bytes=VMEM_LIMIT)


def _proj_kernel(a_ref, b_ref, s_ref, o_ref):
    acc = jnp.dot(a_ref[...], b_ref[...], preferred_element_type=F32)
    o_ref[...] = (acc * s_ref[...]).astype(o_ref.dtype)


def _projection(a, b, colscale, tm, tn):
    m, k = a.shape
    n = b.shape[1]
    tm = min(tm, m)
    return pl.pallas_call(
        _proj_kernel,
        grid=(m // tm, n // tn),
        in_specs=[
            pl.BlockSpec((tm, k), lambda i, j: (i, 0)),
            pl.BlockSpec((k, tn), lambda i, j: (0, j)),
            pl.BlockSpec((1, tn), lambda i, j: (0, j)),
        ],
        out_specs=pl.BlockSpec((tm, tn), lambda i, j: (i, j)),
        out_shape=jax.ShapeDtypeStruct((m, n), BF16),
        compiler_params=_cparams(("parallel", "parallel")),
        name="in_proj",
    )(a, b, colscale)


def _qk_prep_kernel(x_ref, g_ref, c_ref, s_ref, o_ref):
    cos = c_ref[...]
    sin = s_ref[...]
    lane = lax.broadcasted_iota(jnp.int32, cos.shape, 1)
    first_half = (lane % (ROPE_AXIS_DIM)) < (ROPE_AXIS_DIM // 2)
    for hh in range(x_ref.shape[1] // HEAD_DIM):
        sl = slice(hh * HEAD_DIM, (hh + 1) * HEAD_DIM)
        x = x_ref[:, sl].astype(F32)
        ms = jnp.mean(x * x, axis=-1, keepdims=True)
        y = x * lax.rsqrt(ms + RMS_EPS) * g_ref[:, sl]
        partner = jnp.where(first_half,
                            pltpu.roll(y, HEAD_DIM - ROPE_AXIS_DIM // 2, 1),
                            pltpu.roll(y, ROPE_AXIS_DIM // 2, 1))
        o_ref[:, sl] = (y * cos + partner * sin).astype(o_ref.dtype)


def _qk_prep(proj, gains, cos, sin, tm):
    m = proj.shape[0]
    tm = min(tm, m)
    w = B_KV_WIDTH
    first = (3 * A_WIDTH) // w
    nblk = (B_WIDTH + B_KV_WIDTH) // w
    ntab = cos.shape[0] // tm
    return pl.pallas_call(
        _qk_prep_kernel,
        grid=(m // tm, nblk),
        in_specs=[
            pl.BlockSpec((tm, w), lambda i, j: (i, first + j)),
            pl.BlockSpec((None, 1, w), lambda i, j: (j, 0, 0)),
            pl.BlockSpec((tm, HEAD_DIM), lambda i, j: (i % ntab, 0)),
            pl.BlockSpec((tm, HEAD_DIM), lambda i, j: (i % ntab, 0)),
        ],
        out_specs=pl.BlockSpec((tm, w), lambda i, j: (i, j)),
        out_shape=jax.ShapeDtypeStruct((m, B_WIDTH + B_KV_WIDTH), BF16),
        compiler_params=_cparams(("parallel", "parallel")),
        name="qk_prep",
    )(proj, gains, cos, sin)


def _scores_t(k, q):
    return lax.dot_general(k, q, (((1,), (1,)), ((), ())), preferred_element_type=F32)


def _softmax_init(s, v_t, m_ref, l_ref, acc_ref):
    m = jnp.max(s, axis=0, keepdims=True)
    p = jnp.exp2(s - m)
    m_ref[...] = m
    l_ref[...] = jnp.sum(p, axis=0, keepdims=True)
    acc_ref[...] = jnp.dot(v_t, p.astype(v_t.dtype), preferred_element_type=F32)


def _softmax_update(s, shift, v_t, m_ref, l_ref, acc_ref):
    m_prev = m_ref[...]
    m_new = jnp.maximum(m_prev, jnp.max(s, axis=0, keepdims=True) + shift)
    p = jnp.exp2(s - (m_new - shift))
    alpha = jnp.exp2(m_prev - m_new)
    l_ref[...] = alpha * l_ref[...] + jnp.sum(p, axis=0, keepdims=True)
    acc_ref[...] = alpha * acc_ref[...] + jnp.dot(v_t, p.astype(v_t.dtype),
                                                  preferred_element_type=F32)
    m_ref[...] = m_new


def _diff_attn_kernel(c_ref, q1_ref, q2_ref, k1_ref, k2_ref, vt_ref, mk1_ref, mk2_ref, mvt_ref,
                      w_ref, g_ref, o_ref, band_ref, mbias_ref, m_ref, l_ref, acc_ref,
                      sa_ref, sb_ref, *, tk):
    h = pl.program_id(1)
    i = pl.program_id(2)
    tq = q1_ref.shape[0]
    nk = k1_ref.shape[0] // tk
    bias_left = c_ref[h]
    bias_right = c_ref[A_HEADS + h]
    lam = c_ref[2 * A_HEADS]
    q1 = q1_ref[...]
    q2 = q2_ref[...]
    stats = (m_ref, l_ref, acc_ref)

    def scores_into(dst, j):
        rows = pl.ds(pl.multiple_of(j * tk, tk), tk)
        dst[:, :tq] = _scores_t(k1_ref[rows, :], q1)
        dst[:, tq:] = _scores_t(k2_ref[rows, :], q2)

    scores_into(sa_ref, 0)

    @pl.when(i == 0)
    def _():
        width = w_ref.shape[1]
        rolled = pltpu.roll(jnp.broadcast_to(w_ref[...], (tk, width)), 0, 1,
                            stride=1, stride_axis=0)
        band_ref[0] = jnp.full((tk, tq), bias_left, F32)
        for d in range(3):
            band_ref[1 + d] = rolled[:, (3 - d) * tq:(4 - d) * tq]
        band_ref[4] = jnp.full((tk, tq), bias_right, F32)
        mrolled = pltpu.roll(jnp.broadcast_to(w_ref[...], (META_PAD, width)), width - N_META, 1,
                             stride=1, stride_axis=0)
        mbias_ref[...] = mrolled[:, 2 * tq:3 * tq]

    def both(x):
        return jnp.concatenate([x, x], axis=1)

    meta_ok = lax.broadcasted_iota(jnp.int32, (META_PAD, 2 * tq), 0) < N_META
    meta_bias = jnp.where(i == 0, mbias_ref[...], bias_left)
    s = jnp.concatenate([_scores_t(mk1_ref[...], q1), _scores_t(mk2_ref[...], q2)], axis=1)
    _softmax_init(jnp.where(meta_ok, s + both(meta_bias), MASK_VALUE), mvt_ref[...], *stats)

    def consume(src, j):
        bias = band_ref[jnp.clip(j - i, -2, 2) + 2]
        _softmax_update(src[...] + both(bias), 0.0, vt_ref[j], *stats)

    def body(jj, carry):
        j = 2 * jj
        scores_into(sb_ref, j + 1)
        consume(sa_ref, j)
        scores_into(sa_ref, jnp.minimum(j + 2, nk - 1))
        consume(sb_ref, j + 1)
        return carry

    lax.fori_loop(0, nk // 2, body, 0)

    o = acc_ref[...] * (1.0 / l_ref[...])
    o = o[:, :tq] - lam * o[:, tq:]
    ms = jnp.mean(o * o, axis=0, keepdims=True)
    o = o * lax.rsqrt(ms + RMS_EPS) * g_ref[...] * (1.0 - LAMBDA_INIT)
    o_ref[...] = o.T.astype(o_ref.dtype)


def _diff_attention(proj3, vt, meta_proj, meta_vt, w_bias, consts, subln, tq):
    nb, seq, _ = proj3.shape
    dv = 2 * HEAD_DIM
    nk = seq // tq
    kq2 = A_HEADS
    kk1 = A_WIDTH // HEAD_DIM
    kk2 = kk1 + A_HEADS
    assert nk % 2 == 0
    stat = [pltpu.VMEM((1, 2 * tq), F32), pltpu.VMEM((1, 2 * tq), F32), pltpu.VMEM((dv, 2 * tq), F32),
            pltpu.VMEM((tq, 2 * tq), F32), pltpu.VMEM((tq, 2 * tq), F32)]
    return pl.pallas_call(
        functools.partial(_diff_attn_kernel, tk=tq),
        grid=(nb, A_HEADS, nk),
        in_specs=[
            pl.BlockSpec(memory_space=pltpu.SMEM),
            pl.BlockSpec((None, tq, HEAD_DIM), lambda b, h, i: (b, i, h)),
            pl.BlockSpec((None, tq, HEAD_DIM), lambda b, h, i: (b, i, kq2 + h)),
            pl.BlockSpec((None, seq, HEAD_DIM), lambda b, h, i: (b, 0, kk1 + h)),
            pl.BlockSpec((None, seq, HEAD_DIM), lambda b, h, i: (b, 0, kk2 + h)),
            pl.BlockSpec((None, None, nk, dv, tq), lambda b, h, i: (b, h, 0, 0, 0)),
            pl.BlockSpec((META_PAD, HEAD_DIM), lambda b, h, i: (0, kk1 + h)),
            pl.BlockSpec((META_PAD, HEAD_DIM), lambda b, h, i: (0, kk2 + h)),
            pl.BlockSpec((None, dv, META_PAD), lambda b, h, i: (h, 0, 0)),
            pl.BlockSpec((None, 1, 4 * tq), lambda b, h, i: (h, 0, 0)),
            pl.BlockSpec((dv, 1), lambda b, h, i: (0, 0)),
        ],
        out_specs=pl.BlockSpec((None, tq, dv), lambda b, h, i: (b, i, h)),
        out_shape=jax.ShapeDtypeStruct((nb, seq, A_WIDTH), BF16),
        scratch_shapes=[pltpu.VMEM((5, tq, tq), F32), pltpu.VMEM((META_PAD, tq), F32)] + stat,
        compiler_params=_cparams(("parallel", "parallel", "arbitrary")),
        name="diff_attn",
    )(consts, proj3, proj3, proj3, proj3, vt, meta_proj, meta_proj, meta_vt, w_bias, subln)


def _gqa_kernel(q_ref, k_ref, vt_ref, mk_ref, mvt_ref, o_ref, m_ref, l_ref, acc_ref,
                sa_ref, sb_ref, *, tk, heads):
    nk = k_ref.shape[0] // tk
    tq = q_ref.shape[0]
    q = jnp.concatenate([q_ref[:, g * HEAD_DIM:(g + 1) * HEAD_DIM] for g in range(heads)], axis=0)
    stats = (m_ref, l_ref, acc_ref)

    def scores_into(dst, j):
        dst[...] = _scores_t(k_ref[pl.ds(pl.multiple_of(j * tk, tk), tk), :], q)

    scores_into(sa_ref, 0)
    meta_ok = lax.broadcasted_iota(jnp.int32, (META_PAD, heads * tq), 0) < N_META
    s = jnp.where(meta_ok, _scores_t(mk_ref[...], q), MASK_VALUE)
    _softmax_init(s, mvt_ref[...], *stats)

    def body(jj, carry):
        j = 2 * jj
        scores_into(sb_ref, j + 1)
        _softmax_update(sa_ref[...], 0.0, vt_ref[j], *stats)
        scores_into(sa_ref, jnp.minimum(j + 2, nk - 1))
        _softmax_update(sb_ref[...], 0.0, vt_ref[j + 1], *stats)
        return carry

    lax.fori_loop(0, nk // 2, body, 0)
    o = acc_ref[...] * (1.0 / l_ref[...])
    for g in range(heads):
        o_ref[:, g * HEAD_DIM:(g + 1) * HEAD_DIM] = o[:, g * tq:(g + 1) * tq].T.astype(o_ref.dtype)


def _gqa_attention(bqk3, vt, meta_bqk, meta_vt, tq, heads):
    nb, seq, _ = bqk3.shape
    nk = seq // tq
    assert nk % 2 == 0
    kk = B_HEADS
    per_group = B_GROUP // heads
    wide = heads * tq
    stat = [pltpu.VMEM((1, wide), F32), pltpu.VMEM((1, wide), F32), pltpu.VMEM((HEAD_DIM, wide), F32),
            pltpu.VMEM((tq, wide), F32), pltpu.VMEM((tq, wide), F32)]
    return pl.pallas_call(
        functools.partial(_gqa_kernel, tk=tq, heads=heads),
        grid=(nb, B_HEADS // heads, nk),
        in_specs=[
            pl.BlockSpec((None, tq, heads * HEAD_DIM), lambda b, h, i: (b, i, h)),
            pl.BlockSpec((None, seq, HEAD_DIM), lambda b, h, i: (b, 0, kk + h // per_group)),
            pl.BlockSpec((None, None, nk, HEAD_DIM, tq),
                         lambda b, h, i: (b, h // per_group, 0, 0, 0)),
            pl.BlockSpec((META_PAD, HEAD_DIM), lambda b, h, i: (0, kk + h // per_group)),
            pl.BlockSpec((None, HEAD_DIM, META_PAD), lambda b, h, i: (h // per_group, 0, 0)),
        ],
        out_specs=pl.BlockSpec((None, tq, heads * HEAD_DIM), lambda b, h, i: (b, i, h)),
        out_shape=jax.ShapeDtypeStruct((nb, seq, B_WIDTH), BF16),
        scratch_shapes=stat,
        compiler_params=_cparams(("parallel", "parallel", "parallel")),
        name="gqa_attn",
    )(bqk3, bqk3, vt, meta_bqk, meta_vt)


def _chunked_transpose(v3, meta_v, heads, dv, tk):
    nb, seq, _ = v3.shape
    vt = v3.reshape(nb, seq // tk, tk, heads, dv).transpose(0, 3, 1, 4, 2)
    meta_vt = meta_v.reshape(META_PAD, heads, dv).transpose(1, 2, 0)
    return vt, meta_vt


def _layer_norm(y, g, b):
    mu = jnp.mean(y, axis=-1, keepdims=True)
    yc = y - mu
    var = jnp.mean(yc * yc, axis=-1, keepdims=True)
    return yc * lax.rsqrt(var + LN_EPS) * g + b


def _out_proj_ln_kernel(a_ref, b_ref, w_ref, xp_hbm, xs_hbm, g_ref, beta_ref, h_ref, hb_ref,
                        acc_ref, x_ref, sem, *, ka, tiles_p):
    i = pl.program_id(0)
    k = pl.program_id(1)
    tm = acc_ref.shape[0]

    def x_copy(src, tile):
        rows = pl.ds(pl.multiple_of(tile * tm, tm), tm)
        return pltpu.make_async_copy(src.at[rows, :], x_ref, sem.at[0])

    def for_x_source(fn):
        @pl.when(i < tiles_p)
        def _():
            fn(x_copy(xp_hbm, i))

        @pl.when(i >= tiles_p)
        def _():
            fn(x_copy(xs_hbm, i - tiles_p))

    @pl.when(k == 0)
    def _():
        acc_ref[...] = jnp.zeros_like(acc_ref)
        for_x_source(lambda c: c.start())

    @pl.when(k < ka)
    def _():
        acc_ref[...] += jnp.dot(a_ref[...], w_ref[...], preferred_element_type=F32)

    @pl.when(k >= ka)
    def _():
        acc_ref[...] += jnp.dot(b_ref[...], w_ref[...], preferred_element_type=F32)

    @pl.when(k == pl.num_programs(1) - 1)
    def _():
        for_x_source(lambda c: c.wait())

        def body(c, carry):
            sl = pl.ds(pl.multiple_of(c * LN_SLAB, LN_SLAB), LN_SLAB)
            h = _layer_norm(DEEPNORM_ALPHA * x_ref[sl, :] + acc_ref[sl, :], g_ref[...], beta_ref[...])
            h_ref[sl, :] = h
            hb_ref[sl, :] = h.astype(hb_ref.dtype)
            return carry
        lax.fori_loop(0, acc_ref.shape[0] // LN_SLAB, body, 0)


def _out_proj_ln(a_out, b_out, w_out, x_p, x_s, g, beta, tm, tk):
    n_p, d = x_p.shape
    n = n_p + x_s.shape[0]
    tm = math.gcd(math.gcd(tm, n_p), x_s.shape[0])
    ka = a_out.shape[1] // tk
    kb = b_out.shape[1] // tk
    out_block = pl.BlockSpec((tm, d), lambda i, k: (i, 0), pipeline_mode=pl.Buffered(1))
    return pl.pallas_call(
        functools.partial(_out_proj_ln_kernel, ka=ka, tiles_p=n_p // tm),
        grid=(n // tm, ka + kb),
        in_specs=[
            pl.BlockSpec((tm, tk), lambda i, k: (i, jnp.minimum(k, ka - 1))),
            pl.BlockSpec((tm, tk), lambda i, k: (i, jnp.maximum(k - ka, 0))),
            pl.BlockSpec((tk, d), lambda i, k: (k, 0)),
            pl.BlockSpec(memory_space=pl.ANY),
            pl.BlockSpec(memory_space=pl.ANY),
            pl.BlockSpec((1, d), lambda i, k: (0, 0)),
            pl.BlockSpec((1, d), lambda i, k: (0, 0)),
        ],
        out_specs=[out_block, out_block],
        out_shape=[jax.ShapeDtypeStruct((n, d), F32), jax.ShapeDtypeStruct((n, d), BF16)],
        scratch_shapes=[pltpu.VMEM((tm, d), F32), pltpu.VMEM((tm, d), F32),
                        pltpu.SemaphoreType.DMA((1,))],
        compiler_params=_cparams(("parallel", "arbitrary")),
        name="out_proj_ln1",
    )(a_out, b_out, w_out, x_p, x_s, g, beta)


def _router_kernel(h_ref, w_ref, b_ref, o_ref):
    o_ref[...] = jnp.dot(h_ref[...], w_ref[...], precision=lax.Precision.HIGHEST,
                         preferred_element_type=F32) + b_ref[...]


def _router(h, w, b, tm):
    m, d = h.shape
    e = w.shape[1]
    return pl.pallas_call(
        _router_kernel,
        grid=(m // tm,),
        in_specs=[
            pl.BlockSpec((tm, d), lambda i: (i, 0)),
            pl.BlockSpec((d, e), lambda i: (0, 0)),
            pl.BlockSpec((1, e), lambda i: (0, 0)),
        ],
        out_specs=pl.BlockSpec((tm, e), lambda i: (i, 0)),
        out_shape=jax.ShapeDtypeStruct((m, e), F32),
        compiler_params=_cparams(("parallel",)),
        name="router",
    )(h, w, b)


class _Items:
    EXPERT, COL, ROW, IN_ROW, GROUP, FIRST, NEXT_EXPERT, NEXT_COL, HAS_NEXT, COUNT = range(10)
    N = 10


def _stream_weights(s, w_hbm, col_offsets, wbuf, sem, tn):
    g = pl.program_id(0)

    def copies(e, j, slot):
        return [pltpu.make_async_copy(
            w_hbm.at[e, :, pl.ds(pl.multiple_of((off + j) * tn, tn), tn)],
            wbuf.at[slot, c], sem.at[slot, c]) for c, off in enumerate(col_offsets)]

    slot = s[_Items.GROUP][g] % 2

    @pl.when(g == 0)
    def _():
        for c in copies(s[_Items.EXPERT][0], s[_Items.COL][0], 0):
            c.start()

    @pl.when(s[_Items.FIRST][g] == 1)
    def _():
        for c in copies(s[_Items.EXPERT][g], s[_Items.COL][g], slot):
            c.wait()

        @pl.when(s[_Items.HAS_NEXT][g] == 1)
        def _():
            for c in copies(s[_Items.NEXT_EXPERT][g], s[_Items.NEXT_COL][g], 1 - slot):
                c.start()

    return slot


def _ffn1_kernel(*refs, ncol, tn):
    s = refs[:_Items.N]
    x_ref, w_hbm, bg_ref, bl_ref, o_ref, wbuf, sem = refs[_Items.N:]

    @pl.when(pl.program_id(0) < s[_Items.COUNT][0])
    def _():
        slot = _stream_weights(s, w_hbm, (0, ncol), wbuf, sem, tn)
        x = x_ref[...]
        hg = jnp.dot(x, wbuf[slot, 0].astype(BF16), preferred_element_type=F32) + bg_ref[...]
        hl = jnp.dot(x, wbuf[slot, 1].astype(BF16), preferred_element_type=F32) + bl_ref[...]
        hg = jnp.minimum(hg, SWIGLU_LIMIT)
        hl = jnp.clip(hl, -SWIGLU_LIMIT, SWIGLU_LIMIT)
        act = hg * (1.0 / (1.0 + jnp.exp(-SWIGLU_ALPHA * hg))) * (hl + 1.0)
        o_ref[...] = act.astype(o_ref.dtype)

    @pl.when(pl.program_id(0) >= s[_Items.COUNT][0])
    def _():
        o_ref[...] = jnp.zeros_like(o_ref)


def _ffn2_kernel(*refs, tn):
    s = refs[:_Items.N]
    a_ref, w_hbm, b_ref, g_ref, o_ref, wbuf, sem = refs[_Items.N:]

    @pl.when(pl.program_id(0) < s[_Items.COUNT][0])
    def _():
        slot = _stream_weights(s, w_hbm, (0,), wbuf, sem, tn)
        y = jnp.dot(a_ref[...], wbuf[slot, 0].astype(BF16), preferred_element_type=F32) + b_ref[...]
        o_ref[...] = (y * g_ref[...]).astype(o_ref.dtype)

    @pl.when(pl.program_id(0) >= s[_Items.COUNT][0])
    def _():
        o_ref[...] = jnp.zeros_like(o_ref)


def _work_items(tiles_per, ncol, n_items_max):
    i32 = jnp.int32
    experts = jnp.arange(N_EXPERTS, dtype=i32)
    tile_end = jnp.cumsum(tiles_per)
    tile_start = tile_end - tiles_per
    items_per = tiles_per * ncol
    item_end = jnp.cumsum(items_per)
    n_items = item_end[-1:]
    g_all = jnp.arange(n_items_max, dtype=i32)
    fill = g_all - n_items[0]
    g = jnp.minimum(g_all, n_items[0] - 1)
    e = jnp.minimum(jnp.sum((item_end[None, :] <= g[:, None]).astype(i32), axis=1), N_EXPERTS - 1)
    local = g - (item_end - items_per)[e]
    per = jnp.maximum(tiles_per[e], 1)
    j = local // per
    t = local % per
    used = tiles_per > 0
    groups_before = (jnp.cumsum(used.astype(i32)) - used.astype(i32)) * ncol
    later = jnp.where(used[None, :] & (experts[None, :] > experts[:, None]), experts[None, :], N_EXPERTS)
    next_used = jnp.min(later, axis=1)
    last_col = j == ncol - 1
    next_e = jnp.where(last_col, next_used[e], e)
    has_next = next_e < N_EXPERTS
    row = tile_start[e] + t
    return tuple(a.astype(i32) for a in (
        e, jnp.where(fill >= 0, fill % ncol, j), jnp.where(fill >= 0, tile_end[-1] + fill // ncol, row),
        row, groups_before[e] + j, t == 0,
        jnp.minimum(next_e, N_EXPERTS - 1), jnp.where(last_col, 0, j + 1), has_next, n_items))


def _expert_ffn1(x_rows, w1, b1, items, tm, tn):
    n_rows, d = x_rows.shape
    dff = w1.shape[2] // 2
    ncol = dff // tn
    E, C, R = _Items.EXPERT, _Items.COL, _Items.ROW
    grid_spec = pltpu.PrefetchScalarGridSpec(
        num_scalar_prefetch=_Items.N,
        grid=(items[0].shape[0],),
        in_specs=[
            pl.BlockSpec((tm, d), lambda g, *s: (s[_Items.IN_ROW][g], 0)),
            pl.BlockSpec(memory_space=pl.ANY),
            pl.BlockSpec((None, 1, tn), lambda g, *s: (s[E][g], 0, s[C][g])),
            pl.BlockSpec((None, 1, tn), lambda g, *s: (s[E][g], 0, ncol + s[C][g])),
        ],
        out_specs=pl.BlockSpec((tm, tn), lambda g, *s: (s[R][g], s[C][g])),
        scratch_shapes=[pltpu.VMEM((2, 2, d, tn), F32), pltpu.SemaphoreType.DMA((2, 2))],
    )
    return pl.pallas_call(
        functools.partial(_ffn1_kernel, ncol=ncol, tn=tn),
        grid_spec=grid_spec,
        out_shape=jax.ShapeDtypeStruct((n_rows, dff), BF16),
        compiler_params=_cparams(("arbitrary",)),
        name="moe_ffn1",
    )(*items, x_rows, w1, b1, b1)


def _expert_ffn2(act, w2, b2, row_gate, items, tm, tn):
    n_rows, dff = act.shape
    d = w2.shape[2]
    E, C, R = _Items.EXPERT, _Items.COL, _Items.ROW
    grid_spec = pltpu.PrefetchScalarGridSpec(
        num_scalar_prefetch=_Items.N,
        grid=(items[0].shape[0],),
        in_specs=[
            pl.BlockSpec((tm, dff), lambda g, *s: (s[_Items.IN_ROW][g], 0)),
            pl.BlockSpec(memory_space=pl.ANY),
            pl.BlockSpec((None, 1, tn), lambda g, *s: (s[E][g], 0, s[C][g])),
            pl.BlockSpec((tm, 1), lambda g, *s: (s[_Items.IN_ROW][g], 0)),
        ],
        out_specs=pl.BlockSpec((tm, tn), lambda g, *s: (s[R][g], s[C][g])),
        scratch_shapes=[pltpu.VMEM((2, 1, dff, tn), F32), pltpu.SemaphoreType.DMA((2, 1))],
    )
    return pl.pallas_call(
        functools.partial(_ffn2_kernel, tn=tn),
        grid_spec=grid_spec,
        out_shape=jax.ShapeDtypeStruct((n_rows, d), BF16),
        compiler_params=_cparams(("arbitrary",)),
        name="moe_ffn2",
    )(*items, act, w2, b2, row_gate)


def _residual_ln_kernel(h_ref, *refs):
    y_refs, (g_ref, b_ref, o_ref) = refs[:TOP_K], refs[TOP_K:]
    y = y_refs[0][...].astype(F32)
    for y_ref in y_refs[1:]:
        y = y + y_ref[...].astype(F32)
    o_ref[...] = _layer_norm(DEEPNORM_ALPHA * h_ref[...] + y, g_ref[...], b_ref[...])


def _residual_ln(h, ys, g, b, row0, m, tm):
    d = h.shape[1]
    tm = min(tm, m)
    off = row0 // tm
    tile = pl.BlockSpec((tm, d), lambda i: (i + off, 0))
    vec = pl.BlockSpec((1, d), lambda i: (0, 0))
    return pl.pallas_call(
        _residual_ln_kernel,
        grid=(m // tm,),
        in_specs=[tile] * (1 + TOP_K) + [vec, vec],
        out_specs=pl.BlockSpec((tm, d), lambda i: (i, 0)),
        out_shape=jax.ShapeDtypeStruct((m, d), F32),
        compiler_params=_cparams(("parallel",)),
        name="residual_ln2",
    )(h, *ys, g, b)


def _rel_bucket(rel):
    nb = REL_BUCKETS // 2
    max_exact = nb // 2
    ret = jnp.where(rel > 0, nb, 0)
    n = jnp.abs(rel)
    nf = jnp.maximum(n, 1).astype(F32)
    large = max_exact + (jnp.log(nf / max_exact) / math.log(REL_MAX_DIST / max_exact)
                         * (nb - max_exact)).astype(jnp.int32)
    large = jnp.minimum(large, nb - 1)
    return ret + jnp.where(n < max_exact, n, large)


def _bias_tables(rel_bias, tq):
    assert tq >= REL_MAX_DIST
    rel = 2 * tq - jnp.arange(4 * tq, dtype=jnp.int32)
    w = rel_bias[_rel_bucket(rel)].astype(F32).T[:, None, :]
    far = rel_bias[_rel_bucket(jnp.array([-(tq + 1), tq + 1], jnp.int32))].astype(F32)
    return w, far


def _rope_tables(seq):
    tok = jnp.arange(seq, dtype=jnp.int32)
    row_id = (tok // GRID_W).astype(F32)
    col_id = (tok % GRID_W).astype(F32)
    inv = ROPE_THETA ** (-jnp.arange(0, ROPE_AXIS_DIM, 2, dtype=F32) / ROPE_AXIS_DIM)
    ang_r = row_id[:, None] * inv[None, :]
    ang_c = col_id[:, None] * inv[None, :]
    cos = jnp.concatenate([jnp.cos(ang_r)] * 2 + [jnp.cos(ang_c)] * 2, axis=-1)
    sin = jnp.concatenate([-jnp.sin(ang_r), jnp.sin(ang_r), -jnp.sin(ang_c), jnp.sin(ang_c)], axis=-1)
    return cos, sin


def _route(logits, tm, n_tiles):
    n = logits.shape[0]
    top_val, top_idx = lax.top_k(logits, TOP_K)
    gate = jax.nn.softmax(top_val, axis=-1)
    n_assign = n * TOP_K
    e_flat = top_idx.reshape(-1)
    g_flat = gate.reshape(-1)
    experts = jnp.arange(N_EXPERTS, dtype=jnp.int32)
    counts = jnp.sum((experts[:, None] == e_flat[None, :]).astype(jnp.int32), axis=1)
    tiles_per = (counts + tm - 1) // tm
    tile_end = jnp.cumsum(tiles_per)
    pstart = (tile_end - tiles_per) * tm
    start = jnp.cumsum(counts) - counts
    order = jnp.argsort(e_flat).astype(jnp.int32)
    rank = jnp.argsort(order).astype(jnp.int32)
    dest = ((pstart - start)[e_flat] + rank).reshape(n, TOP_K)
    tiles = jnp.arange(n_tiles, dtype=jnp.int32)
    e_tile = jnp.minimum(jnp.sum((tile_end[None, :] <= tiles[:, None]).astype(jnp.int32), axis=1),
                         N_EXPERTS - 1)
    first = start[e_tile] + (tiles - (tile_end - tiles_per)[e_tile]) * tm
    last = (start + counts)[e_tile]
    a = first[:, None] + jnp.arange(tm, dtype=jnp.int32)[None, :]
    real = (a < last[:, None]).reshape(-1)
    src = order[jnp.clip(a, 0, n_assign - 1).reshape(-1)]
    row_tok = jnp.where(real, src // TOP_K, jnp.arange(n_tiles * tm, dtype=jnp.int32) % n)
    row_gate = jnp.where(real, g_flat[src], 0.0)
    return row_tok, row_gate, dest, tiles_per


PROJ_TM, PROJ_TN = 1024, 1024
PREP_TM = 512
ATTN_TQ = 512
GQA_HEADS_PER_STEP = 4
OUT_TM, OUT_TK = 512, 512
LN_SLAB = 64
ROUTER_TM = 512
MOE_TM, MOE_TN = 512, 512
MOE_TN2 = 1024
LN_TM = 256


def kernel(x_prompt, x_sample, meta_tokens, rel_bias, w_in, lambda_q1, lambda_k1, lambda_q2,
           lambda_k2, a_subln, q_norm, k_norm, w_out, ln1_g, ln1_b, w_router, b_router,
           w1, b1, w2, b2, ln2_g, ln2_b):
    d = x_prompt.shape[-1]
    seq = x_prompt.shape[1]
    assert x_sample.shape[1] == seq and seq % GRID_W == 0
    nb_p, nb_s = x_prompt.shape[0], x_sample.shape[0]
    nb = nb_p + nb_s
    x_p, x_s = x_prompt.reshape(-1, d), x_sample.reshape(-1, d)
    n_p = x_p.shape[0]
    n = n_p + x_s.shape[0]
    x_b = jnp.concatenate([x_p.astype(BF16), x_s.astype(BF16)], axis=0)
    tq = min(ATTN_TQ, seq)

    w_in_b = w_in[0].astype(BF16)
    colscale = jnp.concatenate([jnp.full((A_WIDTH,), QUERY_SCALE, F32),
                                jnp.ones((IN_COLS - A_WIDTH,), F32)])[None]
    tn = min(PROJ_TN, IN_COLS)
    proj = _projection(x_b, w_in_b, colscale, PROJ_TM, tn)
    meta_rows = jnp.zeros((META_PAD, d), F32).at[:N_META].set(meta_tokens)
    meta_proj = _projection(meta_rows.astype(BF16), w_in_b, colscale, META_PAD, tn)

    gains = jnp.concatenate([jnp.tile(q_norm[0] * QUERY_SCALE, B_HEADS),
                             jnp.tile(k_norm[0], B_KV_HEADS)]).reshape(-1, 1, B_KV_WIDTH)
    cos, sin = _rope_tables(seq)
    bqk = _qk_prep(proj, gains, cos, sin, min(PREP_TM, seq))
    meta_bqk = _qk_prep(meta_proj, gains, jnp.ones((META_PAD, HEAD_DIM), F32),
                        jnp.zeros((META_PAD, HEAD_DIM), F32), META_PAD)

    lam = (jnp.exp(jnp.sum(lambda_q1[0].astype(F32) * lambda_k1[0].astype(F32)))
           - jnp.exp(jnp.sum(lambda_q2[0].astype(F32) * lambda_k2[0].astype(F32))) + LAMBDA_INIT)
    w_bias, far = _bias_tables(rel_bias * LOG2E, tq)
    consts = jnp.concatenate([far[0], far[1], lam[None]]).astype(F32)
    proj3 = proj.reshape(nb, seq, IN_COLS)
    a_v0, b_v0 = 2 * A_WIDTH, 3 * A_WIDTH + B_WIDTH + B_KV_WIDTH
    vt_a, meta_vt_a = _chunked_transpose(proj3[:, :, a_v0:a_v0 + A_WIDTH],
                                         meta_proj[:, a_v0:a_v0 + A_WIDTH], A_HEADS, 2 * HEAD_DIM, tq)
    vt_b, meta_vt_b = _chunked_transpose(proj3[:, :, b_v0:], meta_proj[:, b_v0:],
                                         B_KV_HEADS, HEAD_DIM, tq)
    a_out = _diff_attention(proj3, vt_a, meta_proj, meta_vt_a, w_bias, consts,
                            a_subln.astype(F32).reshape(-1, 1), tq)
    b_out = _gqa_attention(bqk.reshape(nb, seq, -1), vt_b, meta_bqk, meta_vt_b, tq,
                           GQA_HEADS_PER_STEP)

    a2, b2d, w_out_b = a_out.reshape(n, A_WIDTH), b_out.reshape(n, B_WIDTH), w_out[0].astype(BF16)
    h1, h1_b = _out_proj_ln(a2, b2d, w_out_b, x_p, x_s, ln1_g, ln1_b, OUT_TM, OUT_TK)

    logits = _router(h1, w_router[0], b_router, min(ROUTER_TM, n))
    tm = min(MOE_TM, n)
    n_tiles = (n * TOP_K + N_EXPERTS * (tm - 1)) // tm
    row_tok, row_gate, dest, tiles_per = _route(logits, tm, n_tiles)
    x_rows = h1_b[row_tok]
    dff = w2.shape[2]
    tn1, tn2 = min(MOE_TN, dff), min(MOE_TN2, d)
    act = _expert_ffn1(x_rows, w1[0], b1[0].reshape(N_EXPERTS, 1, -1),
                       _work_items(tiles_per, dff // tn1, n_tiles * (dff // tn1)), tm, tn1)
    y_rows = _expert_ffn2(act, w2[0], b2[0].reshape(N_EXPERTS, 1, -1), row_gate[:, None],
                          _work_items(tiles_per, d // tn2, n_tiles * (d // tn2)), tm, tn2)
    ys = [y_rows[dest[:, k]] for k in range(TOP_K)]

    out_p = _residual_ln(h1, ys, ln2_g, ln2_b, 0, n_p, LN_TM)
    out_s = _residual_ln(h1, ys, ln2_g, ln2_b, n_p, n - n_p, LN_TM)
    return (out_p.reshape(nb_p, seq, d), out_s.reshape(nb_s, seq, d))
```

```python
import functools
import math

import jax
import jax.numpy as jnp
from jax import lax
from jax.experimental import pallas as pl
from jax.experimental.pallas import tpu as pltpu

HEAD_DIM = 128
N_META = 16
GRID_W = 64
A_HEADS = 8
A_WIDTH = A_HEADS * 2 * HEAD_DIM
B_HEADS = 16
B_KV_HEADS = 4
B_GROUP = B_HEADS // B_KV_HEADS
B_WIDTH = B_HEADS * HEAD_DIM
B_KV_WIDTH = B_KV_HEADS * HEAD_DIM
IN_COLS = 3 * A_WIDTH + B_WIDTH + 2 * B_KV_WIDTH
ROPE_THETA = 10000.0
ROPE_AXIS_DIM = HEAD_DIM // 2
REL_BUCKETS = 32
REL_MAX_DIST = 128
N_EXPERTS = 32
TOP_K = 4
SWIGLU_ALPHA = 1.702
SWIGLU_LIMIT = 7.0
DEPTH = 1
DEEPNORM_ALPHA = (2 * DEPTH) ** 0.25
LN_EPS = 1e-5
RMS_EPS = 1e-6
LAMBDA_INIT = 0.8 - 0.6 * math.exp(-0.3 * 0)
LOG2E = math.log2(math.e)
QUERY_SCALE = HEAD_DIM ** -0.5 * LOG2E

LANES = 128
META_PAD = 128
MASK_VALUE = -1e30
VMEM_LIMIT = 56 * 1024 * 1024

BF16 = jnp.bfloat16
F32 = jnp.float32


def _cparams(sem):
    return pltpu.CompilerParams(dimension_semantics=sem, vmem_limit_bytes=VMEM_LIMIT)


def _proj_kernel(a_ref, b_ref, s_ref, o_ref):
    acc = jnp.dot(a_ref[...], b_ref[...], preferred_element_type=F32)
    o_ref[...] = (acc * s_ref[...]).astype(o_ref.dtype)


def _projection(a, b, colscale, tm, tn):
    m, k = a.shape
    n = b.shape[1]
    tm = min(tm, m)
    return pl.pallas_call(
        _proj_kernel,
        grid=(m // tm, n // tn),
        in_specs=[
            pl.BlockSpec((tm, k), lambda i, j: (i, 0)),
            pl.BlockSpec((k, tn), lambda i, j: (0, j)),
            pl.BlockSpec((1, tn), lambda i, j: (0, j)),
        ],
        out_specs=pl.BlockSpec((tm, tn), lambda i, j: (i, j)),
        out_shape=jax.ShapeDtypeStruct((m, n), BF16),
        compiler_params=_cparams(("parallel", "parallel")),
        name="in_proj",
    )(a, b, colscale)


def _qk_prep_kernel(x_ref, g_ref, c_ref, s_ref, o_ref):
    cos = c_ref[...]
    sin = s_ref[...]
    lane = lax.broadcasted_iota(jnp.int32, cos.shape, 1)
    first_half = (lane % (ROPE_AXIS_DIM)) < (ROPE_AXIS_DIM // 2)
    for hh in range(x_ref.shape[1] // HEAD_DIM):
        sl = slice(hh * HEAD_DIM, (hh + 1) * HEAD_DIM)
        x = x_ref[:, sl].astype(F32)
        ms = jnp.mean(x * x, axis=-1, keepdims=True)
        y = x * lax.rsqrt(ms + RMS_EPS) * g_ref[:, sl]
        partner = jnp.where(first_half,
                            pltpu.roll(y, HEAD_DIM - ROPE_AXIS_DIM // 2, 1),
                            pltpu.roll(y, ROPE_AXIS_DIM // 2, 1))
        o_ref[:, sl] = (y * cos + partner * sin).astype(o_ref.dtype)


def _qk_prep(proj, gains, cos, sin, tm):
    m = proj.shape[0]
    tm = min(tm, m)
    w = B_KV_WIDTH
    first = (3 * A_WIDTH) // w
    nblk = (B_WIDTH + B_KV_WIDTH) // w
    ntab = cos.shape[0] // tm
    return pl.pallas_call(
        _qk_prep_kernel,
        grid=(m // tm, nblk),
        in_specs=[
            pl.BlockSpec((tm, w), lambda i, j: (i, first + j)),
            pl.BlockSpec((None, 1, w), lambda i, j: (j, 0, 0)),
            pl.BlockSpec((tm, HEAD_DIM), lambda i, j: (i % ntab, 0)),
            pl.BlockSpec((tm, HEAD_DIM), lambda i, j: (i % ntab, 0)),
        ],
        out_specs=pl.BlockSpec((tm, w), lambda i, j: (i, j)),
        out_shape=jax.ShapeDtypeStruct((m, B_WIDTH + B_KV_WIDTH), BF16),
        compiler_params=_cparams(("parallel", "parallel")),
        name="qk_prep",
    )(proj, gains, cos, sin)


def _scores_t(k, q):
    return lax.dot_general(k, q, (((1,), (1,)), ((), ())), preferred_element_type=F32)


def _softmax_init(s, v_t, m_ref, l_ref, acc_ref):
    m = jnp.max(s, axis=0, keepdims=True)
    p = jnp.exp2(s - m)
    m_ref[...] = m
    l_ref[...] = jnp.sum(p, axis=0, keepdims=True)
    acc_ref[...] = jnp.dot(v_t, p.astype(v_t.dtype), preferred_element_type=F32)


def _softmax_update(s, shift, v_t, m_ref, l_ref, acc_ref):
    m_prev = m_ref[...]
    m_new = jnp.maximum(m_prev, jnp.max(s, axis=0, keepdims=True) + shift)
    p = jnp.exp2(s - (m_new - shift))
    alpha = jnp.exp2(m_prev - m_new)
    l_ref[...] = alpha * l_ref[...] + jnp.sum(p, axis=0, keepdims=True)
    acc_ref[...] = alpha * acc_ref[...] + jnp.dot(v_t, p.astype(v_t.dtype),
                                                  preferred_element_type=F32)
    m_ref[...] = m_new


def _diff_attn_kernel(c_ref, q1_ref, q2_ref, k1_ref, k2_ref, vt_ref, mk1_ref, mk2_ref, mvt_ref,
                      w_ref, g_ref, o_ref, band_ref, mbias_ref, m_ref, l_ref, acc_ref,
                      sa_ref, sb_ref, *, tk):
    h = pl.program_id(1)
    i = pl.program_id(2)
    tq = q1_ref.shape[0]
    nk = k1_ref.shape[0] // tk
    bias_left = c_ref[h]
    bias_right = c_ref[A_HEADS + h]
    lam = c_ref[2 * A_HEADS]
    q1 = q1_ref[...]
    q2 = q2_ref[...]
    stats = (m_ref, l_ref, acc_ref)

    def scores_into(dst, j):
        rows = pl.ds(pl.multiple_of(j * tk, tk), tk)
        dst[:, :tq] = _scores_t(k1_ref[rows, :], q1)
        dst[:, tq:] = _scores_t(k2_ref[rows, :], q2)

    scores_into(sa_ref, 0)

    units = tq // tk

    @pl.when(i == 0)
    def _():
        width = w_ref.shape[1]
        rolled = pltpu.roll(jnp.broadcast_to(w_ref[...], (tk, width)), 0, 1,
                            stride=1, stride_axis=0)
        band_ref[0] = jnp.full((tk, tk), bias_left, F32)
        for d in range(3):
            band_ref[1 + d] = rolled[:, (3 - d) * tk:(4 - d) * tk]
        band_ref[4] = jnp.full((tk, tk), bias_right, F32)
        mrolled = pltpu.roll(jnp.broadcast_to(w_ref[...], (META_PAD, width)), width - N_META, 1,
                             stride=1, stride_axis=0)
        mbias_ref[...] = mrolled[:, 2 * tk:3 * tk]

    def both(blocks):
        return jnp.concatenate(blocks * 2, axis=1)

    meta_ok = lax.broadcasted_iota(jnp.int32, (META_PAD, 2 * tq), 0) < N_META
    meta_bias = ([jnp.where(i == 0, mbias_ref[...], bias_left)]
                 + [jnp.full((META_PAD, tk), bias_left, F32)] * (units - 1))
    s = jnp.concatenate([_scores_t(mk1_ref[...], q1), _scores_t(mk2_ref[...], q2)], axis=1)
    _softmax_init(jnp.where(meta_ok, s + both(meta_bias), MASK_VALUE), mvt_ref[...], *stats)

    def consume(src, j):
        bias = [band_ref[jnp.clip(j - (units * i + u), -2, 2) + 2] for u in range(units)]
        _softmax_update(src[...] + both(bias), 0.0, vt_ref[j], *stats)

    def body(jj, carry):
        j = 2 * jj
        scores_into(sb_ref, j + 1)
        consume(sa_ref, j)
        scores_into(sa_ref, jnp.minimum(j + 2, nk - 1))
        consume(sb_ref, j + 1)
        return carry

    lax.fori_loop(0, nk // 2, body, 0)

    o = acc_ref[...] * (1.0 / l_ref[...])
    o = o[:, :tq] - lam * o[:, tq:]
    ms = jnp.mean(o * o, axis=0, keepdims=True)
    o = o * lax.rsqrt(ms + RMS_EPS) * g_ref[...] * (1.0 - LAMBDA_INIT)
    o_ref[...] = o.T.astype(o_ref.dtype)


def _diff_attention(proj3, vt, meta_proj, meta_vt, w_bias, consts, subln, tq, tk):
    nb, seq, _ = proj3.shape
    dv = 2 * HEAD_DIM
    nk = seq // tk
    kq2 = A_HEADS
    kk1 = A_WIDTH // HEAD_DIM
    kk2 = kk1 + A_HEADS
    assert nk % 2 == 0 and tq % tk == 0
    stat = [pltpu.VMEM((1, 2 * tq), F32), pltpu.VMEM((1, 2 * tq), F32), pltpu.VMEM((dv, 2 * tq), F32),
            pltpu.VMEM((tk, 2 * tq), F32), pltpu.VMEM((tk, 2 * tq), F32)]
    return pl.pallas_call(
        functools.partial(_diff_attn_kernel, tk=tk),
        grid=(nb, A_HEADS, seq // tq),
        in_specs=[
            pl.BlockSpec(memory_space=pltpu.SMEM),
            pl.BlockSpec((None, tq, HEAD_DIM), lambda b, h, i: (b, i, h)),
            pl.BlockSpec((None, tq, HEAD_DIM), lambda b, h, i: (b, i, kq2 + h)),
            pl.BlockSpec((None, seq, HEAD_DIM), lambda b, h, i: (b, 0, kk1 + h)),
            pl.BlockSpec((None, seq, HEAD_DIM), lambda b, h, i: (b, 0, kk2 + h)),
            pl.BlockSpec((None, None, nk, dv, tk), lambda b, h, i: (b, h, 0, 0, 0)),
            pl.BlockSpec((META_PAD, HEAD_DIM), lambda b, h, i: (0, kk1 + h)),
            pl.BlockSpec((META_PAD, HEAD_DIM), lambda b, h, i: (0, kk2 + h)),
            pl.BlockSpec((None, dv, META_PAD), lambda b, h, i: (h, 0, 0)),
            pl.BlockSpec((None, 1, 4 * tk), lambda b, h, i: (h, 0, 0)),
            pl.BlockSpec((dv, 1), lambda b, h, i: (0, 0)),
        ],
        out_specs=pl.BlockSpec((None, tq, dv), lambda b, h, i: (b, i, h)),
        out_shape=jax.ShapeDtypeStruct((nb, seq, A_WIDTH), BF16),
        scratch_shapes=[pltpu.VMEM((5, tk, tk), F32), pltpu.VMEM((META_PAD, tk), F32)] + stat,
        compiler_params=_cparams(("parallel", "parallel", "arbitrary")),
        name="diff_attn",
    )(consts, proj3, proj3, proj3, proj3, vt, meta_proj, meta_proj, meta_vt, w_bias, subln)


def _gqa_kernel(q_ref, k_ref, vt_ref, mk_ref, mvt_ref, o_ref, m_ref, l_ref, acc_ref,
                sa_ref, sb_ref, *, tk, heads):
    nk = k_ref.shape[0] // tk
    tq = q_ref.shape[0]
    q = jnp.concatenate([q_ref[:, g * HEAD_DIM:(g + 1) * HEAD_DIM] for g in range(heads)], axis=0)
    stats = (m_ref, l_ref, acc_ref)

    def scores_into(dst, j):
        dst[...] = _scores_t(k_ref[pl.ds(pl.multiple_of(j * tk, tk), tk), :], q)

    scores_into(sa_ref, 0)
    meta_ok = lax.broadcasted_iota(jnp.int32, (META_PAD, heads * tq), 0) < N_META
    s = jnp.where(meta_ok, _scores_t(mk_ref[...], q), MASK_VALUE)
    _softmax_init(s, mvt_ref[...], *stats)

    def body(jj, carry):
        j = 2 * jj
        scores_into(sb_ref, j + 1)
        _softmax_update(sa_ref[...], 0.0, vt_ref[j], *stats)
        scores_into(sa_ref, jnp.minimum(j + 2, nk - 1))
        _softmax_update(sb_ref[...], 0.0, vt_ref[j + 1], *stats)
        return carry

    lax.fori_loop(0, nk // 2, body, 0)
    o = acc_ref[...] * (1.0 / l_ref[...])
    for g in range(heads):
        o_ref[:, g * HEAD_DIM:(g + 1) * HEAD_DIM] = o[:, g * tq:(g + 1) * tq].T.astype(o_ref.dtype)


def _gqa_attention(bqk3, vt, meta_bqk, meta_vt, tq, tk, heads):
    nb, seq, _ = bqk3.shape
    nk = seq // tk
    assert nk % 2 == 0
    kk = B_HEADS
    per_group = B_GROUP // heads
    wide = heads * tq
    stat = [pltpu.VMEM((1, wide), F32), pltpu.VMEM((1, wide), F32), pltpu.VMEM((HEAD_DIM, wide), F32),
            pltpu.VMEM((tk, wide), F32), pltpu.VMEM((tk, wide), F32)]
    return pl.pallas_call(
        functools.partial(_gqa_kernel, tk=tk, heads=heads),
        grid=(nb, B_HEADS // heads, seq // tq),
        in_specs=[
            pl.BlockSpec((None, tq, heads * HEAD_DIM), lambda b, h, i: (b, i, h)),
            pl.BlockSpec((None, seq, HEAD_DIM), lambda b, h, i: (b, 0, kk + h // per_group)),
            pl.BlockSpec((None, None, nk, HEAD_DIM, tk),
                         lambda b, h, i: (b, h // per_group, 0, 0, 0)),
            pl.BlockSpec((META_PAD, HEAD_DIM), lambda b, h, i: (0, kk + h // per_group)),
            pl.BlockSpec((None, HEAD_DIM, META_PAD), lambda b, h, i: (h // per_group, 0, 0)),
        ],
        out_specs=pl.BlockSpec((None, tq, heads * HEAD_DIM), lambda b, h, i: (b, i, h)),
        out_shape=jax.ShapeDtypeStruct((nb, seq, B_WIDTH), BF16),
        scratch_shapes=stat,
        compiler_params=_cparams(("parallel", "parallel", "parallel")),
        name="gqa_attn",
    )(bqk3, bqk3, vt, meta_bqk, meta_vt)


def _chunked_transpose(v3, meta_v, heads, dv, tk):
    nb, seq, _ = v3.shape
    vt = v3.reshape(nb, seq // tk, tk, heads, dv).transpose(0, 3, 1, 4, 2)
    meta_vt = meta_v.reshape(META_PAD, heads, dv).transpose(1, 2, 0)
    return vt, meta_vt


def _layer_norm(y, g, b):
    mu = jnp.mean(y, axis=-1, keepdims=True)
    yc = y - mu
    var = jnp.mean(yc * yc, axis=-1, keepdims=True)
    return yc * lax.rsqrt(var + LN_EPS) * g + b


def _out_proj_ln_kernel(a_ref, b_ref, w_ref, xp_hbm, xs_hbm, g_ref, beta_ref, h_ref, hb_ref,
                        acc_ref, x_ref, sem, *, ka, tiles_p):
    i = pl.program_id(0)
    k = pl.program_id(1)
    tm = acc_ref.shape[0]

    def x_copy(src, tile):
        rows = pl.ds(pl.multiple_of(tile * tm, tm), tm)
        return pltpu.make_async_copy(src.at[rows, :], x_ref, sem.at[0])

    def for_x_source(fn):
        @pl.when(i < tiles_p)
        def _():
            fn(x_copy(xp_hbm, i))

        @pl.when(i >= tiles_p)
        def _():
            fn(x_copy(xs_hbm, i - tiles_p))

    @pl.when(k == 0)
    def _():
        acc_ref[...] = jnp.zeros_like(acc_ref)
        for_x_source(lambda c: c.start())

    @pl.when(k < ka)
    def _():
        acc_ref[...] += jnp.dot(a_ref[...], w_ref[...], preferred_element_type=F32)

    @pl.when(k >= ka)
    def _():
        acc_ref[...] += jnp.dot(b_ref[...], w_ref[...], preferred_element_type=F32)

    @pl.when(k == pl.num_programs(1) - 1)
    def _():
        for_x_source(lambda c: c.wait())

        def body(c, carry):
            sl = pl.ds(pl.multiple_of(c * LN_SLAB, LN_SLAB), LN_SLAB)
            h = _layer_norm(DEEPNORM_ALPHA * x_ref[sl, :] + acc_ref[sl, :], g_ref[...], beta_ref[...])
            h_ref[sl, :] = h
            hb_ref[sl, :] = h.astype(hb_ref.dtype)
            return carry
        lax.fori_loop(0, acc_ref.shape[0] // LN_SLAB, body, 0)


def _out_proj_ln(a_out, b_out, w_out, x_p, x_s, g, beta, tm, tk):
    n_p, d = x_p.shape
    n = n_p + x_s.shape[0]
    tm = math.gcd(math.gcd(tm, n_p), x_s.shape[0])
    ka = a_out.shape[1] // tk
    kb = b_out.shape[1] // tk
    out_block = pl.BlockSpec((tm, d), lambda i, k: (i, 0), pipeline_mode=pl.Buffered(1))
    return pl.pallas_call(
        functools.partial(_out_proj_ln_kernel, ka=ka, tiles_p=n_p // tm),
        grid=(n // tm, ka + kb),
        in_specs=[
            pl.BlockSpec((tm, tk), lambda i, k: (i, jnp.minimum(k, ka - 1))),
            pl.BlockSpec((tm, tk), lambda i, k: (i, jnp.maximum(k - ka, 0))),
            pl.BlockSpec((tk, d), lambda i, k: (k, 0)),
            pl.BlockSpec(memory_space=pl.ANY),
            pl.BlockSpec(memory_space=pl.ANY),
            pl.BlockSpec((1, d), lambda i, k: (0, 0)),
            pl.BlockSpec((1, d), lambda i, k: (0, 0)),
        ],
        out_specs=[out_block, out_block],
        out_shape=[jax.ShapeDtypeStruct((n, d), F32), jax.ShapeDtypeStruct((n, d), BF16)],
        scratch_shapes=[pltpu.VMEM((tm, d), F32), pltpu.VMEM((tm, d), F32),
                        pltpu.SemaphoreType.DMA((1,))],
        compiler_params=_cparams(("parallel", "arbitrary")),
        name="out_proj_ln1",
    )(a_out, b_out, w_out, x_p, x_s, g, beta)


def _router_kernel(h_ref, w_ref, b_ref, o_ref):
    o_ref[...] = jnp.dot(h_ref[...], w_ref[...], precision=lax.Precision.HIGHEST,
                         preferred_element_type=F32) + b_ref[...]


def _router(h, w, b, tm):
    m, d = h.shape
    e = w.shape[1]
    return pl.pallas_call(
        _router_kernel,
        grid=(m // tm,),
        in_specs=[
            pl.BlockSpec((tm, d), lambda i: (i, 0)),
            pl.BlockSpec((d, e), lambda i: (0, 0)),
            pl.BlockSpec((1, e), lambda i: (0, 0)),
        ],
        out_specs=pl.BlockSpec((tm, e), lambda i: (i, 0)),
        out_shape=jax.ShapeDtypeStruct((m, e), F32),
        compiler_params=_cparams(("parallel",)),
        name="router",
    )(h, w, b)


class _Items:
    EXPERT, COL, ROW, IN_ROW, GROUP, FIRST, NEXT_EXPERT, NEXT_COL, HAS_NEXT, COUNT = range(10)
    N = 10


def _stream_weights(s, w_hbm, col_offsets, wbuf, sem, tn):
    g = pl.program_id(0)

    def copies(e, j, slot):
        return [pltpu.make_async_copy(
            w_hbm.at[e, :, pl.ds(pl.multiple_of((off + j) * tn, tn), tn)],
            wbuf.at[slot, c], sem.at[slot, c]) for c, off in enumerate(col_offsets)]

    slot = s[_Items.GROUP][g] % 2

    @pl.when(g == 0)
    def _():
        for c in copies(s[_Items.EXPERT][0], s[_Items.COL][0], 0):
            c.start()

    @pl.when(s[_Items.FIRST][g] == 1)
    def _():
        for c in copies(s[_Items.EXPERT][g], s[_Items.COL][g], slot):
            c.wait()

        @pl.when(s[_Items.HAS_NEXT][g] == 1)
        def _():
            for c in copies(s[_Items.NEXT_EXPERT][g], s[_Items.NEXT_COL][g], 1 - slot):
                c.start()

    return slot


def _ffn1_kernel(*refs, ncol, tn):
    s = refs[:_Items.N]
    x_ref, w_hbm, bg_ref, bl_ref, o_ref, wbuf, sem = refs[_Items.N:]

    @pl.when(pl.program_id(0) < s[_Items.COUNT][0])
    def _():
        slot = _stream_weights(s, w_hbm, (0, ncol), wbuf, sem, tn)
        x = x_ref[...]
        hg = jnp.dot(x, wbuf[slot, 0].astype(BF16), preferred_element_type=F32) + bg_ref[...]
        hl = jnp.dot(x, wbuf[slot, 1].astype(BF16), preferred_element_type=F32) + bl_ref[...]
        hg = jnp.minimum(hg, SWIGLU_LIMIT)
        hl = jnp.clip(hl, -SWIGLU_LIMIT, SWIGLU_LIMIT)
        act = hg * (1.0 / (1.0 + jnp.exp(-SWIGLU_ALPHA * hg))) * (hl + 1.0)
        o_ref[...] = act.astype(o_ref.dtype)

    @pl.when(pl.program_id(0) >= s[_Items.COUNT][0])
    def _():
        o_ref[...] = jnp.zeros_like(o_ref)


def _ffn2_kernel(*refs, tn):
    s = refs[:_Items.N]
    a_ref, w_hbm, b_ref, g_ref, o_ref, wbuf, sem = refs[_Items.N:]

    @pl.when(pl.program_id(0) < s[_Items.COUNT][0])
    def _():
        slot = _stream_weights(s, w_hbm, (0,), wbuf, sem, tn)
        y = jnp.dot(a_ref[...], wbuf[slot, 0].astype(BF16), preferred_element_type=F32) + b_ref[...]
        o_ref[...] = (y * g_ref[...]).astype(o_ref.dtype)

    @pl.when(pl.program_id(0) >= s[_Items.COUNT][0])
    def _():
        o_ref[...] = jnp.zeros_like(o_ref)


def _work_items(tiles_per, ncol, n_items_max):
    i32 = jnp.int32
    experts = jnp.arange(N_EXPERTS, dtype=i32)
    tile_end = jnp.cumsum(tiles_per)
    tile_start = tile_end - tiles_per
    items_per = tiles_per * ncol
    item_end = jnp.cumsum(items_per)
    n_items = item_end[-1:]
    g_all = jnp.arange(n_items_max, dtype=i32)
    fill = g_all - n_items[0]
    g = jnp.minimum(g_all, n_items[0] - 1)
    e = jnp.minimum(jnp.sum((item_end[None, :] <= g[:, None]).astype(i32), axis=1), N_EXPERTS - 1)
    local = g - (item_end - items_per)[e]
    per = jnp.maximum(tiles_per[e], 1)
    j = local // per
    t = local % per
    used = tiles_per > 0
    groups_before = (jnp.cumsum(used.astype(i32)) - used.astype(i32)) * ncol
    later = jnp.where(used[None, :] & (experts[None, :] > experts[:, None]), experts[None, :], N_EXPERTS)
    next_used = jnp.min(later, axis=1)
    last_col = j == ncol - 1
    next_e = jnp.where(last_col, next_used[e], e)
    has_next = next_e < N_EXPERTS
    row = tile_start[e] + t
    return tuple(a.astype(i32) for a in (
        e, jnp.where(fill >= 0, fill % ncol, j), jnp.where(fill >= 0, tile_end[-1] + fill // ncol, row),
        row, groups_before[e] + j, t == 0,
        jnp.minimum(next_e, N_EXPERTS - 1), jnp.where(last_col, 0, j + 1), has_next, n_items))


def _expert_ffn1(x_rows, w1, b1, items, tm, tn):
    n_rows, d = x_rows.shape
    dff = w1.shape[2] // 2
    ncol = dff // tn
    E, C, R = _Items.EXPERT, _Items.COL, _Items.ROW
    grid_spec = pltpu.PrefetchScalarGridSpec(
        num_scalar_prefetch=_Items.N,
        grid=(items[0].shape[0],),
        in_specs=[
            pl.BlockSpec((tm, d), lambda g, *s: (s[_Items.IN_ROW][g], 0)),
            pl.BlockSpec(memory_space=pl.ANY),
            pl.BlockSpec((None, 1, tn), lambda g, *s: (s[E][g], 0, s[C][g])),
            pl.BlockSpec((None, 1, tn), lambda g, *s: (s[E][g], 0, ncol + s[C][g])),
        ],
        out_specs=pl.BlockSpec((tm, tn), lambda g, *s: (s[R][g], s[C][g])),
        scratch_shapes=[pltpu.VMEM((2, 2, d, tn), F32), pltpu.SemaphoreType.DMA((2, 2))],
    )
    return pl.pallas_call(
        functools.partial(_ffn1_kernel, ncol=ncol, tn=tn),
        grid_spec=grid_spec,
        out_shape=jax.ShapeDtypeStruct((n_rows, dff), BF16),
        compiler_params=_cparams(("arbitrary",)),
        name="moe_ffn1",
    )(*items, x_rows, w1, b1, b1)


def _expert_ffn2(act, w2, b2, row_gate, items, tm, tn):
    n_rows, dff = act.shape
    d = w2.shape[2]
    E, C, R = _Items.EXPERT, _Items.COL, _Items.ROW
    grid_spec = pltpu.PrefetchScalarGridSpec(
        num_scalar_prefetch=_Items.N,
        grid=(items[0].shape[0],),
        in_specs=[
            pl.BlockSpec((tm, dff), lambda g, *s: (s[_Items.IN_ROW][g], 0)),
            pl.BlockSpec(memory_space=pl.ANY),
            pl.BlockSpec((None, 1, tn), lambda g, *s: (s[E][g], 0, s[C][g])),
            pl.BlockSpec((tm, 1), lambda g, *s: (s[_Items.IN_ROW][g], 0)),
        ],
        out_specs=pl.BlockSpec((tm, tn), lambda g, *s: (s[R][g], s[C][g])),
        scratch_shapes=[pltpu.VMEM((2, 1, dff, tn), F32), pltpu.SemaphoreType.DMA((2, 1))],
    )
    return pl.pallas_call(
        functools.partial(_ffn2_kernel, tn=tn),
        grid_spec=grid_spec,
        out_shape=jax.ShapeDtypeStruct((n_rows, d), BF16),
        compiler_params=_cparams(("arbitrary",)),
        name="moe_ffn2",
    )(*items, act, w2, b2, row_gate)


def _residual_ln_kernel(h_ref, *refs):
    y_refs, (g_ref, b_ref, o_ref) = refs[:TOP_K], refs[TOP_K:]
    y = y_refs[0][...].astype(F32)
    for y_ref in y_refs[1:]:
        y = y + y_ref[...].astype(F32)
    o_ref[...] = _layer_norm(DEEPNORM_ALPHA * h_ref[...] + y, g_ref[...], b_ref[...])


def _residual_ln(h, ys, g, b, row0, m, tm):
    d = h.shape[1]
    tm = min(tm, m)
    off = row0 // tm
    tile = pl.BlockSpec((tm, d), lambda i: (i + off, 0))
    vec = pl.BlockSpec((1, d), lambda i: (0, 0))
    return pl.pallas_call(
        _residual_ln_kernel,
        grid=(m // tm,),
        in_specs=[tile] * (1 + TOP_K) + [vec, vec],
        out_specs=pl.BlockSpec((tm, d), lambda i: (i, 0)),
        out_shape=jax.ShapeDtypeStruct((m, d), F32),
        compiler_params=_cparams(("parallel",)),
        name="residual_ln2",
    )(h, *ys, g, b)


def _rel_bucket(rel):
    nb = REL_BUCKETS // 2
    max_exact = nb // 2
    ret = jnp.where(rel > 0, nb, 0)
    n = jnp.abs(rel)
    nf = jnp.maximum(n, 1).astype(F32)
    large = max_exact + (jnp.log(nf / max_exact) / math.log(REL_MAX_DIST / max_exact)
                         * (nb - max_exact)).astype(jnp.int32)
    large = jnp.minimum(large, nb - 1)
    return ret + jnp.where(n < max_exact, n, large)


def _bias_tables(rel_bias, tq):
    assert tq >= REL_MAX_DIST
    rel = 2 * tq - jnp.arange(4 * tq, dtype=jnp.int32)
    w = rel_bias[_rel_bucket(rel)].astype(F32).T[:, None, :]
    far = rel_bias[_rel_bucket(jnp.array([-(tq + 1), tq + 1], jnp.int32))].astype(F32)
    return w, far


def _rope_tables(seq):
    tok = jnp.arange(seq, dtype=jnp.int32)
    row_id = (tok // GRID_W).astype(F32)
    col_id = (tok % GRID_W).astype(F32)
    inv = ROPE_THETA ** (-jnp.arange(0, ROPE_AXIS_DIM, 2, dtype=F32) / ROPE_AXIS_DIM)
    ang_r = row_id[:, None] * inv[None, :]
    ang_c = col_id[:, None] * inv[None, :]
    cos = jnp.concatenate([jnp.cos(ang_r)] * 2 + [jnp.cos(ang_c)] * 2, axis=-1)
    sin = jnp.concatenate([-jnp.sin(ang_r), jnp.sin(ang_r), -jnp.sin(ang_c), jnp.sin(ang_c)], axis=-1)
    return cos, sin


def _route(logits, tm, n_tiles):
    n = logits.shape[0]
    top_val, top_idx = lax.top_k(logits, TOP_K)
    gate = jax.nn.softmax(top_val, axis=-1)
    n_assign = n * TOP_K
    e_flat = top_idx.reshape(-1)
    g_flat = gate.reshape(-1)
    experts = jnp.arange(N_EXPERTS, dtype=jnp.int32)
    counts = jnp.sum((experts[:, None] == e_flat[None, :]).astype(jnp.int32), axis=1)
    tiles_per = (counts + tm - 1) // tm
    tile_end = jnp.cumsum(tiles_per)
    pstart = (tile_end - tiles_per) * tm
    start = jnp.cumsum(counts) - counts
    order = jnp.argsort(e_flat).astype(jnp.int32)
    rank = jnp.argsort(order).astype(jnp.int32)
    dest = ((pstart - start)[e_flat] + rank).reshape(n, TOP_K)
    tiles = jnp.arange(n_tiles, dtype=jnp.int32)
    e_tile = jnp.minimum(jnp.sum((tile_end[None, :] <= tiles[:, None]).astype(jnp.int32), axis=1),
                         N_EXPERTS - 1)
    first = start[e_tile] + (tiles - (tile_end - tiles_per)[e_tile]) * tm
    last = (start + counts)[e_tile]
    a = first[:, None] + jnp.arange(tm, dtype=jnp.int32)[None, :]
    real = (a < last[:, None]).reshape(-1)
    src = order[jnp.clip(a, 0, n_assign - 1).reshape(-1)]
    row_tok = jnp.where(real, src // TOP_K, jnp.arange(n_tiles * tm, dtype=jnp.int32) % n)
    row_gate = jnp.where(real, g_flat[src], 0.0)
    return row_tok, row_gate, dest, tiles_per


PROJ_TM, PROJ_TN = 1024, 1024
PREP_TM = 512
ATTN_TQ = 512
GQA_HEADS_PER_STEP = 4
GQA_TQ = 1024
DIFF_TQ = 1024
OUT_TM, OUT_TK = 512, 512
LN_SLAB = 64
ROUTER_TM = 512
MOE_TM, MOE_TN = 512, 512
MOE_TN2 = 1024
LN_TM = 256


def kernel(x_prompt, x_sample, meta_tokens, rel_bias, w_in, lambda_q1, lambda_k1, lambda_q2,
           lambda_k2, a_subln, q_norm, k_norm, w_out, ln1_g, ln1_b, w_router, b_router,
           w1, b1, w2, b2, ln2_g, ln2_b):
    d = x_prompt.shape[-1]
    seq = x_prompt.shape[1]
    assert x_sample.shape[1] == seq and seq % GRID_W == 0
    nb_p, nb_s = x_prompt.shape[0], x_sample.shape[0]
    nb = nb_p + nb_s
    x_p, x_s = x_prompt.reshape(-1, d), x_sample.reshape(-1, d)
    n_p = x_p.shape[0]
    n = n_p + x_s.shape[0]
    x_b = jnp.concatenate([x_p.astype(BF16), x_s.astype(BF16)], axis=0)
    tq = min(ATTN_TQ, seq)

    w_in_b = w_in[0].astype(BF16)
    colscale = jnp.concatenate([jnp.full((A_WIDTH,), QUERY_SCALE, F32),
                                jnp.ones((IN_COLS - A_WIDTH,), F32)])[None]
    tn = min(PROJ_TN, IN_COLS)
    proj = _projection(x_b, w_in_b, colscale, PROJ_TM, tn)
    meta_rows = jnp.zeros((META_PAD, d), F32).at[:N_META].set(meta_tokens)
    meta_proj = _projection(meta_rows.astype(BF16), w_in_b, colscale, META_PAD, tn)

    gains = jnp.concatenate([jnp.tile(q_norm[0] * QUERY_SCALE, B_HEADS),
                             jnp.tile(k_norm[0], B_KV_HEADS)]).reshape(-1, 1, B_KV_WIDTH)
    cos, sin = _rope_tables(seq)
    bqk = _qk_prep(proj, gains, cos, sin, min(PREP_TM, seq))
    meta_bqk = _qk_prep(meta_proj, gains, jnp.ones((META_PAD, HEAD_DIM), F32),
                        jnp.zeros((META_PAD, HEAD_DIM), F32), META_PAD)

    lam = (jnp.exp(jnp.sum(lambda_q1[0].astype(F32) * lambda_k1[0].astype(F32)))
           - jnp.exp(jnp.sum(lambda_q2[0].astype(F32) * lambda_k2[0].astype(F32))) + LAMBDA_INIT)
    w_bias, far = _bias_tables(rel_bias * LOG2E, tq)
    consts = jnp.concatenate([far[0], far[1], lam[None]]).astype(F32)
    proj3 = proj.reshape(nb, seq, IN_COLS)
    a_v0, b_v0 = 2 * A_WIDTH, 3 * A_WIDTH + B_WIDTH + B_KV_WIDTH
    vt_a, meta_vt_a = _chunked_transpose(proj3[:, :, a_v0:a_v0 + A_WIDTH],
                                         meta_proj[:, a_v0:a_v0 + A_WIDTH], A_HEADS, 2 * HEAD_DIM, tq)
    vt_b, meta_vt_b = _chunked_transpose(proj3[:, :, b_v0:], meta_proj[:, b_v0:],
                                         B_KV_HEADS, HEAD_DIM, tq)
    a_out = _diff_attention(proj3, vt_a, meta_proj, meta_vt_a, w_bias, consts,
                            a_subln.astype(F32).reshape(-1, 1), min(DIFF_TQ, seq), tq)
    b_out = _gqa_attention(bqk.reshape(nb, seq, -1), vt_b, meta_bqk, meta_vt_b,
                           min(GQA_TQ, seq), tq, GQA_HEADS_PER_STEP)

    a2, b2d, w_out_b = a_out.reshape(n, A_WIDTH), b_out.reshape(n, B_WIDTH), w_out[0].astype(BF16)
    h1, h1_b = _out_proj_ln(a2, b2d, w_out_b, x_p, x_s, ln1_g, ln1_b, OUT_TM, OUT_TK)

    logits = _router(h1, w_router[0], b_router, min(ROUTER_TM, n))
    tm = min(MOE_TM, n)
    n_tiles = (n * TOP_K + N_EXPERTS * (tm - 1)) // tm
    row_tok, row_gate, dest, tiles_per = _route(logits, tm, n_tiles)
    x_rows = h1_b[row_tok]
    dff = w2.shape[2]
    tn1, tn2 = min(MOE_TN, dff), min(MOE_TN2, d)
    act = _expert_ffn1(x_rows, w1[0], b1[0].reshape(N_EXPERTS, 1, -1),
                       _work_items(tiles_per, dff // tn1, n_tiles * (dff // tn1)), tm, tn1)
    y_rows = _expert_ffn2(act, w2[0], b2[0].reshape(N_EXPERTS, 1, -1), row_gate[:, None],
                          _work_items(tiles_per, d // tn2, n_tiles * (d // tn2)), tm, tn2)
    ys = [y_rows[dest[:, k]] for k in range(TOP_K)]

    out_p = _residual_ln(h1, ys, ln2_g, ln2_b, 0, n_p, LN_TM)
    out_s = _residual_ln(h1, ys, ln2_g, ln2_b, n_p, n - n_p, LN_TM)
    return (out_p.reshape(nb_p, seq, d), out_s.reshape(nb_s, seq, d))
```

```python
import functools
import math

import jax
import jax.numpy as jnp
from jax import lax
from jax.experimental import pallas as pl
from jax.experimental.pallas import tpu as pltpu

HEAD_DIM = 128
N_META = 16
GRID_W = 64
A_HEADS = 8
A_WIDTH = A_HEADS * 2 * HEAD_DIM
B_HEADS = 16
B_KV_HEADS = 4
B_GROUP = B_HEADS // B_KV_HEADS
B_WIDTH = B_HEADS * HEAD_DIM
B_KV_WIDTH = B_KV_HEADS * HEAD_DIM
IN_COLS = 3 * A_WIDTH + B_WIDTH + 2 * B_KV_WIDTH
ROPE_THETA = 10000.0
ROPE_AXIS_DIM = HEAD_DIM // 2
REL_BUCKETS = 32
REL_MAX_DIST = 128
N_EXPERTS = 32
TOP_K = 4
SWIGLU_ALPHA = 1.702
SWIGLU_LIMIT = 7.0
DEPTH = 1
DEEPNORM_ALPHA = (2 * DEPTH) ** 0.25
LN_EPS = 1e-5
RMS_EPS = 1e-6
LAMBDA_INIT = 0.8 - 0.6 * math.exp(-0.3 * 0)
LOG2E = math.log2(math.e)
QUERY_SCALE = HEAD_DIM ** -0.5 * LOG2E

LANES = 128
META_PAD = 128
MASK_VALUE = -1e30
VMEM_LIMIT = 56 * 1024 * 1024

BF16 = jnp.bfloat16
F32 = jnp.float32


def _cparams(sem):
    return pltpu.CompilerParams(dimension_semantics=sem, vmem_limit_bytes=VMEM_LIMIT)


def _proj_kernel(a_ref, b_ref, s_ref, o_ref):
    acc = jnp.dot(a_ref[...], b_ref[...], preferred_element_type=F32)
    o_ref[...] = (acc * s_ref[...]).astype(o_ref.dtype)


def _projection(a, b, colscale, tm, tn):
    m, k = a.shape
    n = b.shape[1]
    tm = min(tm, m)
    return pl.pallas_call(
        _proj_kernel,
        grid=(m // tm, n // tn),
        in_specs=[
            pl.BlockSpec((tm, k), lambda i, j: (i, 0)),
            pl.BlockSpec((k, tn), lambda i, j: (0, j)),
            pl.BlockSpec((1, tn), lambda i, j: (0, j)),
        ],
        out_specs=pl.BlockSpec((tm, tn), lambda i, j: (i, j)),
        out_shape=jax.ShapeDtypeStruct((m, n), BF16),
        compiler_params=_cparams(("parallel", "parallel")),
        name="in_proj",
    )(a, b, colscale)


def _qk_prep_kernel(x_ref, g_ref, c_ref, s_ref, o_ref):
    cos = c_ref[...]
    sin = s_ref[...]
    lane = lax.broadcasted_iota(jnp.int32, cos.shape, 1)
    first_half = (lane % (ROPE_AXIS_DIM)) < (ROPE_AXIS_DIM // 2)
    for hh in range(x_ref.shape[1] // HEAD_DIM):
        sl = slice(hh * HEAD_DIM, (hh + 1) * HEAD_DIM)
        x = x_ref[:, sl].astype(F32)
        ms = jnp.mean(x * x, axis=-1, keepdims=True)
        y = x * lax.rsqrt(ms + RMS_EPS) * g_ref[:, sl]
        partner = jnp.where(first_half,
                            pltpu.roll(y, HEAD_DIM - ROPE_AXIS_DIM // 2, 1),
                            pltpu.roll(y, ROPE_AXIS_DIM // 2, 1))
        o_ref[:, sl] = (y * cos + partner * sin).astype(o_ref.dtype)


def _qk_prep(proj, gains, cos, sin, tm):
    m = proj.shape[0]
    tm = min(tm, m)
    w = B_KV_WIDTH
    first = (3 * A_WIDTH) // w
    nblk = (B_WIDTH + B_KV_WIDTH) // w
    ntab = cos.shape[0] // tm
    return pl.pallas_call(
        _qk_prep_kernel,
        grid=(m // tm, nblk),
        in_specs=[
            pl.BlockSpec((tm, w), lambda i, j: (i, first + j)),
            pl.BlockSpec((None, 1, w), lambda i, j: (j, 0, 0)),
            pl.BlockSpec((tm, HEAD_DIM), lambda i, j: (i % ntab, 0)),
            pl.BlockSpec((tm, HEAD_DIM), lambda i, j: (i % ntab, 0)),
        ],
        out_specs=pl.BlockSpec((tm, w), lambda i, j: (i, j)),
        out_shape=jax.ShapeDtypeStruct((m, B_WIDTH + B_KV_WIDTH), BF16),
        compiler_params=_cparams(("parallel", "parallel")),
        name="qk_prep",
    )(proj, gains, cos, sin)


def _scores_t(k, q):
    return lax.dot_general(k, q, (((1,), (1,)), ((), ())), preferred_element_type=F32)


def _softmax_init(s, v_t, m_ref, l_ref, acc_ref):
    m = jnp.max(s, axis=0, keepdims=True)
    p = jnp.exp2(s - m)
    m_ref[...] = m
    l_ref[...] = jnp.sum(p, axis=0, keepdims=True)
    acc_ref[...] = jnp.dot(v_t, p.astype(v_t.dtype), preferred_element_type=F32)


def _softmax_update(s, shift, v_t, m_ref, l_ref, acc_ref):
    m_prev = m_ref[...]
    m_new = jnp.maximum(m_prev, jnp.max(s, axis=0, keepdims=True) + shift)
    p = jnp.exp2(s - (m_new - shift))
    alpha = jnp.exp2(m_prev - m_new)
    l_ref[...] = alpha * l_ref[...] + jnp.sum(p, axis=0, keepdims=True)
    acc_ref[...] = alpha * acc_ref[...] + jnp.dot(v_t, p.astype(v_t.dtype),
                                                  preferred_element_type=F32)
    m_ref[...] = m_new


def _diff_attn_kernel(c_ref, q1_ref, q2_ref, k1_ref, k2_ref, vt_ref, mk1_ref, mk2_ref, mvt_ref,
                      w_ref, g_ref, o_ref, band_ref, mbias_ref, m_ref, l_ref, acc_ref,
                      sa_ref, sb_ref, *, tk):
    h = pl.program_id(1)
    i = pl.program_id(2)
    tq = q1_ref.shape[0]
    nk = k1_ref.shape[0] // tk
    bias_left = c_ref[h]
    bias_right = c_ref[A_HEADS + h]
    lam = c_ref[2 * A_HEADS]
    q1 = q1_ref[...]
    q2 = q2_ref[...]
    stats = (m_ref, l_ref, acc_ref)

    def scores_into(dst, j):
        rows = pl.ds(pl.multiple_of(j * tk, tk), tk)
        dst[:, :tq] = _scores_t(k1_ref[rows, :], q1)
        dst[:, tq:] = _scores_t(k2_ref[rows, :], q2)

    scores_into(sa_ref, 0)

    units = tq // tk

    @pl.when(i == 0)
    def _():
        width = w_ref.shape[1]
        rolled = pltpu.roll(jnp.broadcast_to(w_ref[...], (tk, width)), 0, 1,
                            stride=1, stride_axis=0)
        band_ref[0] = jnp.full((tk, tk), bias_left, F32)
        for d in range(3):
            band_ref[1 + d] = rolled[:, (3 - d) * tk:(4 - d) * tk]
        band_ref[4] = jnp.full((tk, tk), bias_right, F32)
        mrolled = pltpu.roll(jnp.broadcast_to(w_ref[...], (META_PAD, width)), width - N_META, 1,
                             stride=1, stride_axis=0)
        mbias_ref[...] = mrolled[:, 2 * tk:3 * tk]

    def both(blocks):
        return jnp.concatenate(blocks * 2, axis=1)

    meta_ok = lax.broadcasted_iota(jnp.int32, (META_PAD, 2 * tq), 0) < N_META
    meta_bias = ([jnp.where(i == 0, mbias_ref[...], bias_left)]
                 + [jnp.full((META_PAD, tk), bias_left, F32)] * (units - 1))
    s = jnp.concatenate([_scores_t(mk1_ref[...], q1), _scores_t(mk2_ref[...], q2)], axis=1)
    _softmax_init(jnp.where(meta_ok, s + both(meta_bias), MASK_VALUE), mvt_ref[...], *stats)

    def consume(src, j):
        bias = [band_ref[jnp.clip(j - (units * i + u), -2, 2) + 2] for u in range(units)]
        _softmax_update(src[...] + both(bias), 0.0, vt_ref[j], *stats)

    def body(jj, carry):
        j = 2 * jj
        scores_into(sb_ref, j + 1)
        consume(sa_ref, j)
        scores_into(sa_ref, jnp.minimum(j + 2, nk - 1))
        consume(sb_ref, j + 1)
        return carry

    lax.fori_loop(0, nk // 2, body, 0)

    o = acc_ref[...] * (1.0 / l_ref[...])
    o = o[:, :tq] - lam * o[:, tq:]
    ms = jnp.mean(o * o, axis=0, keepdims=True)
    o = o * lax.rsqrt(ms + RMS_EPS) * g_ref[...] * (1.0 - LAMBDA_INIT)
    o_ref[...] = o.T.astype(o_ref.dtype)


def _diff_attention(proj3, vt, meta_proj, meta_vt, w_bias, consts, subln, tq, tk):
    nb, seq, _ = proj3.shape
    dv = 2 * HEAD_DIM
    nk = seq // tk
    kq2 = A_HEADS
    kk1 = A_WIDTH // HEAD_DIM
    kk2 = kk1 + A_HEADS
    assert nk % 2 == 0 and tq % tk == 0
    stat = [pltpu.VMEM((1, 2 * tq), F32), pltpu.VMEM((1, 2 * tq), F32), pltpu.VMEM((dv, 2 * tq), F32),
            pltpu.VMEM((tk, 2 * tq), F32), pltpu.VMEM((tk, 2 * tq), F32)]
    return pl.pallas_call(
        functools.partial(_diff_attn_kernel, tk=tk),
        grid=(nb, A_HEADS, seq // tq),
        in_specs=[
            pl.BlockSpec(memory_space=pltpu.SMEM),
            pl.BlockSpec((None, tq, HEAD_DIM), lambda b, h, i: (b, i, h)),
            pl.BlockSpec((None, tq, HEAD_DIM), lambda b, h, i: (b, i, kq2 + h)),
            pl.BlockSpec((None, seq, HEAD_DIM), lambda b, h, i: (b, 0, kk1 + h)),
            pl.BlockSpec((None, seq, HEAD_DIM), lambda b, h, i: (b, 0, kk2 + h)),
            pl.BlockSpec((None, None, nk, dv, tk), lambda b, h, i: (b, h, 0, 0, 0)),
            pl.BlockSpec((META_PAD, HEAD_DIM), lambda b, h, i: (0, kk1 + h)),
            pl.BlockSpec((META_PAD, HEAD_DIM), lambda b, h, i: (0, kk2 + h)),
            pl.BlockSpec((None, dv, META_PAD), lambda b, h, i: (h, 0, 0)),
            pl.BlockSpec((None, 1, 4 * tk), lambda b, h, i: (h, 0, 0)),
            pl.BlockSpec((dv, 1), lambda b, h, i: (0, 0)),
        ],
        out_specs=pl.BlockSpec((None, tq, dv), lambda b, h, i: (b, i, h)),
        out_shape=jax.ShapeDtypeStruct((nb, seq, A_WIDTH), BF16),
        scratch_shapes=[pltpu.VMEM((5, tk, tk), F32), pltpu.VMEM((META_PAD, tk), F32)] + stat,
        compiler_params=_cparams(("parallel", "parallel", "arbitrary")),
        name="diff_attn",
    )(consts, proj3, proj3, proj3, proj3, vt, meta_proj, meta_proj, meta_vt, w_bias, subln)


def _gqa_kernel(q_ref, k_ref, vt_ref, mk_ref, mvt_ref, o_ref, m_ref, l_ref, acc_ref,
                sa_ref, sb_ref, *, tk, heads):
    nk = k_ref.shape[0] // tk
    tq = q_ref.shape[0]
    q = jnp.concatenate([q_ref[:, g * HEAD_DIM:(g + 1) * HEAD_DIM] for g in range(heads)], axis=0)
    stats = (m_ref, l_ref, acc_ref)

    def scores_into(dst, j):
        dst[...] = _scores_t(k_ref[pl.ds(pl.multiple_of(j * tk, tk), tk), :], q)

    scores_into(sa_ref, 0)
    meta_ok = lax.broadcasted_iota(jnp.int32, (META_PAD, heads * tq), 0) < N_META
    s = jnp.where(meta_ok, _scores_t(mk_ref[...], q), MASK_VALUE)
    _softmax_init(s, mvt_ref[...], *stats)

    def body(jj, carry):
        j = 2 * jj
        scores_into(sb_ref, j + 1)
        _softmax_update(sa_ref[...], 0.0, vt_ref[j], *stats)
        scores_into(sa_ref, jnp.minimum(j + 2, nk - 1))
        _softmax_update(sb_ref[...], 0.0, vt_ref[j + 1], *stats)
        return carry

    lax.fori_loop(0, nk // 2, body, 0)
    o = acc_ref[...] * (1.0 / l_ref[...])
    for g in range(heads):
        o_ref[:, g * HEAD_DIM:(g + 1) * HEAD_DIM] = o[:, g * tq:(g + 1) * tq].T.astype(o_ref.dtype)


def _gqa_attention(bqk3, vt, meta_bqk, meta_vt, tq, tk, heads):
    nb, seq, _ = bqk3.shape
    nk = seq // tk
    assert nk % 2 == 0
    kk = B_HEADS
    per_group = B_GROUP // heads
    wide = heads * tq
    stat = [pltpu.VMEM((1, wide), F32), pltpu.VMEM((1, wide), F32), pltpu.VMEM((HEAD_DIM, wide), F32),
            pltpu.VMEM((tk, wide), F32), pltpu.VMEM((tk, wide), F32)]
    return pl.pallas_call(
        functools.partial(_gqa_kernel, tk=tk, heads=heads),
        grid=(nb, B_HEADS // heads, seq // tq),
        in_specs=[
            pl.BlockSpec((None, tq, heads * HEAD_DIM), lambda b, h, i: (b, i, h)),
            pl.BlockSpec((None, seq, HEAD_DIM), lambda b, h, i: (b, 0, kk + h // per_group)),
            pl.BlockSpec((None, None, nk, HEAD_DIM, tk),
                         lambda b, h, i: (b, h // per_group, 0, 0, 0)),
            pl.BlockSpec((META_PAD, HEAD_DIM), lambda b, h, i: (0, kk + h // per_group)),
            pl.BlockSpec((None, HEAD_DIM, META_PAD), lambda b, h, i: (h // per_group, 0, 0)),
        ],
        out_specs=pl.BlockSpec((None, tq, heads * HEAD_DIM), lambda b, h, i: (b, i, h)),
        out_shape=jax.ShapeDtypeStruct((nb, seq, B_WIDTH), BF16),
        scratch_shapes=stat,
        compiler_params=_cparams(("parallel", "parallel", "parallel")),
        name="gqa_attn",
    )(bqk3, bqk3, vt, meta_bqk, meta_vt)


def _chunked_transpose(v3, meta_v, heads, dv, tk):
    nb, seq, _ = v3.shape
    vt = v3.reshape(nb, seq // tk, tk, heads, dv).transpose(0, 3, 1, 4, 2)
    meta_vt = meta_v.reshape(META_PAD, heads, dv).transpose(1, 2, 0)
    return vt, meta_vt


def _layer_norm(y, g, b):
    mu = jnp.mean(y, axis=-1, keepdims=True)
    yc = y - mu
    var = jnp.mean(yc * yc, axis=-1, keepdims=True)
    return yc * lax.rsqrt(var + LN_EPS) * g + b


def _out_proj_ln_kernel(a_ref, b_ref, w_ref, xp_hbm, xs_hbm, g_ref, beta_ref, wr_ref, br_ref,
                        h_ref, hb_ref, lg_ref, acc_ref, x_ref, sem, *, ka, tiles_p):
    i = pl.program_id(0)
    k = pl.program_id(1)
    tm = acc_ref.shape[0]

    def x_copy(src, tile):
        rows = pl.ds(pl.multiple_of(tile * tm, tm), tm)
        return pltpu.make_async_copy(src.at[rows, :], x_ref, sem.at[0])

    def for_x_source(fn):
        @pl.when(i < tiles_p)
        def _():
            fn(x_copy(xp_hbm, i))

        @pl.when(i >= tiles_p)
        def _():
            fn(x_copy(xs_hbm, i - tiles_p))

    @pl.when(k == 0)
    def _():
        acc_ref[...] = jnp.zeros_like(acc_ref)
        for_x_source(lambda c: c.start())

    @pl.when(k < ka)
    def _():
        acc_ref[...] += jnp.dot(a_ref[...], w_ref[...], preferred_element_type=F32)

    @pl.when(k >= ka)
    def _():
        acc_ref[...] += jnp.dot(b_ref[...], w_ref[...], preferred_element_type=F32)

    @pl.when(k == pl.num_programs(1) - 1)
    def _():
        for_x_source(lambda c: c.wait())

        def body(c, carry):
            sl = pl.ds(pl.multiple_of(c * LN_SLAB, LN_SLAB), LN_SLAB)
            h = _layer_norm(DEEPNORM_ALPHA * x_ref[sl, :] + acc_ref[sl, :], g_ref[...], beta_ref[...])
            h_ref[sl, :] = h
            hb_ref[sl, :] = h.astype(hb_ref.dtype)
            lg_ref[sl, :] = jnp.dot(h, wr_ref[...], precision=lax.Precision.HIGHEST,
                                    preferred_element_type=F32) + br_ref[...]
            return carry
        lax.fori_loop(0, acc_ref.shape[0] // LN_SLAB, body, 0)


def _out_proj_ln(a_out, b_out, w_out, x_p, x_s, g, beta, w_router, b_router, tm, tk):
    n_p, d = x_p.shape
    n_exp = w_router.shape[1]
    n = n_p + x_s.shape[0]
    tm = math.gcd(math.gcd(tm, n_p), x_s.shape[0])
    ka = a_out.shape[1] // tk
    kb = b_out.shape[1] // tk
    out_block = pl.BlockSpec((tm, d), lambda i, k: (i, 0), pipeline_mode=pl.Buffered(1))
    return pl.pallas_call(
        functools.partial(_out_proj_ln_kernel, ka=ka, tiles_p=n_p // tm),
        grid=(n // tm, ka + kb),
        in_specs=[
            pl.BlockSpec((tm, tk), lambda i, k: (i, jnp.minimum(k, ka - 1))),
            pl.BlockSpec((tm, tk), lambda i, k: (i, jnp.maximum(k - ka, 0))),
            pl.BlockSpec((tk, d), lambda i, k: (k, 0)),
            pl.BlockSpec(memory_space=pl.ANY),
            pl.BlockSpec(memory_space=pl.ANY),
            pl.BlockSpec((1, d), lambda i, k: (0, 0)),
            pl.BlockSpec((1, d), lambda i, k: (0, 0)),
            pl.BlockSpec((d, n_exp), lambda i, k: (0, 0)),
            pl.BlockSpec((1, n_exp), lambda i, k: (0, 0)),
        ],
        out_specs=[out_block, out_block, pl.BlockSpec((tm, n_exp), lambda i, k: (i, 0))],
        out_shape=[jax.ShapeDtypeStruct((n, d), F32), jax.ShapeDtypeStruct((n, d), BF16),
                   jax.ShapeDtypeStruct((n, n_exp), F32)],
        scratch_shapes=[pltpu.VMEM((tm, d), F32), pltpu.VMEM((tm, d), F32),
                        pltpu.SemaphoreType.DMA((1,))],
        compiler_params=_cparams(("parallel", "arbitrary")),
        name="out_proj_ln1",
    )(a_out, b_out, w_out, x_p, x_s, g, beta, w_router, b_router)


class _Items:
    EXPERT, COL, ROW, IN_ROW, GROUP, FIRST, NEXT_EXPERT, NEXT_COL, HAS_NEXT, COUNT = range(10)
    N = 10


def _stream_weights(s, w_hbm, col_offsets, wbuf, sem, tn):
    g = pl.program_id(0)

    def copies(e, j, slot):
        return [pltpu.make_async_copy(
            w_hbm.at[e, :, pl.ds(pl.multiple_of((off + j) * tn, tn), tn)],
            wbuf.at[slot, c], sem.at[slot, c]) for c, off in enumerate(col_offsets)]

    slot = s[_Items.GROUP][g] % 2

    @pl.when(g == 0)
    def _():
        for c in copies(s[_Items.EXPERT][0], s[_Items.COL][0], 0):
            c.start()

    @pl.when(s[_Items.FIRST][g] == 1)
    def _():
        for c in copies(s[_Items.EXPERT][g], s[_Items.COL][g], slot):
            c.wait()

        @pl.when(s[_Items.HAS_NEXT][g] == 1)
        def _():
            for c in copies(s[_Items.NEXT_EXPERT][g], s[_Items.NEXT_COL][g], 1 - slot):
                c.start()

    return slot


def _ffn1_kernel(*refs, ncol, tn):
    s = refs[:_Items.N]
    x_ref, w_hbm, bg_ref, bl_ref, o_ref, wbuf, sem = refs[_Items.N:]

    @pl.when(pl.program_id(0) < s[_Items.COUNT][0])
    def _():
        slot = _stream_weights(s, w_hbm, (0, ncol), wbuf, sem, tn)
        x = x_ref[...]
        hg = jnp.dot(x, wbuf[slot, 0].astype(BF16), preferred_element_type=F32) + bg_ref[...]
        hl = jnp.dot(x, wbuf[slot, 1].astype(BF16), preferred_element_type=F32) + bl_ref[...]
        hg = jnp.minimum(hg, SWIGLU_LIMIT)
        hl = jnp.clip(hl, -SWIGLU_LIMIT, SWIGLU_LIMIT)
        act = hg * (1.0 / (1.0 + jnp.exp(-SWIGLU_ALPHA * hg))) * (hl + 1.0)
        o_ref[...] = act.astype(o_ref.dtype)

    @pl.when(pl.program_id(0) >= s[_Items.COUNT][0])
    def _():
        o_ref[...] = jnp.zeros_like(o_ref)


def _ffn2_kernel(*refs, tn):
    s = refs[:_Items.N]
    a_ref, w_hbm, b_ref, g_ref, o_ref, wbuf, sem = refs[_Items.N:]

    @pl.when(pl.program_id(0) < s[_Items.COUNT][0])
    def _():
        slot = _stream_weights(s, w_hbm, (0,), wbuf, sem, tn)
        y = jnp.dot(a_ref[...], wbuf[slot, 0].astype(BF16), preferred_element_type=F32) + b_ref[...]
        o_ref[...] = (y * g_ref[...]).astype(o_ref.dtype)

    @pl.when(pl.program_id(0) >= s[_Items.COUNT][0])
    def _():
        o_ref[...] = jnp.zeros_like(o_ref)


def _work_items(tiles_per, ncol, n_items_max):
    i32 = jnp.int32
    experts = jnp.arange(N_EXPERTS, dtype=i32)
    tile_end = jnp.cumsum(tiles_per)
    tile_start = tile_end - tiles_per
    items_per = tiles_per * ncol
    item_end = jnp.cumsum(items_per)
    n_items = item_end[-1:]
    g_all = jnp.arange(n_items_max, dtype=i32)
    fill = g_all - n_items[0]
    g = jnp.minimum(g_all, n_items[0] - 1)
    e = jnp.minimum(jnp.sum((item_end[None, :] <= g[:, None]).astype(i32), axis=1), N_EXPERTS - 1)
    local = g - (item_end - items_per)[e]
    per = jnp.maximum(tiles_per[e], 1)
    j = local // per
    t = local % per
    used = tiles_per > 0
    groups_before = (jnp.cumsum(used.astype(i32)) - used.astype(i32)) * ncol
    later = jnp.where(used[None, :] & (experts[None, :] > experts[:, None]), experts[None, :], N_EXPERTS)
    next_used = jnp.min(later, axis=1)
    last_col = j == ncol - 1
    next_e = jnp.where(last_col, next_used[e], e)
    has_next = next_e < N_EXPERTS
    row = tile_start[e] + t
    return tuple(a.astype(i32) for a in (
        e, jnp.where(fill >= 0, fill % ncol, j), jnp.where(fill >= 0, tile_end[-1] + fill // ncol, row),
        row, groups_before[e] + j, t == 0,
        jnp.minimum(next_e, N_EXPERTS - 1), jnp.where(last_col, 0, j + 1), has_next, n_items))


def _expert_ffn1(x_rows, w1, b1, items, tm, tn):
    n_rows, d = x_rows.shape
    dff = w1.shape[2] // 2
    ncol = dff // tn
    E, C, R = _Items.EXPERT, _Items.COL, _Items.ROW
    grid_spec = pltpu.PrefetchScalarGridSpec(
        num_scalar_prefetch=_Items.N,
        grid=(items[0].shape[0],),
        in_specs=[
            pl.BlockSpec((tm, d), lambda g, *s: (s[_Items.IN_ROW][g], 0)),
            pl.BlockSpec(memory_space=pl.ANY),
            pl.BlockSpec((None, 1, tn), lambda g, *s: (s[E][g], 0, s[C][g])),
            pl.BlockSpec((None, 1, tn), lambda g, *s: (s[E][g], 0, ncol + s[C][g])),
        ],
        out_specs=pl.BlockSpec((tm, tn), lambda g, *s: (s[R][g], s[C][g])),
        scratch_shapes=[pltpu.VMEM((2, 2, d, tn), F32), pltpu.SemaphoreType.DMA((2, 2))],
    )
    return pl.pallas_call(
        functools.partial(_ffn1_kernel, ncol=ncol, tn=tn),
        grid_spec=grid_spec,
        out_shape=jax.ShapeDtypeStruct((n_rows, dff), BF16),
        compiler_params=_cparams(("arbitrary",)),
        name="moe_ffn1",
    )(*items, x_rows, w1, b1, b1)


def _expert_ffn2(act, w2, b2, row_gate, items, tm, tn):
    n_rows, dff = act.shape
    d = w2.shape[2]
    E, C, R = _Items.EXPERT, _Items.COL, _Items.ROW
    grid_spec = pltpu.PrefetchScalarGridSpec(
        num_scalar_prefetch=_Items.N,
        grid=(items[0].shape[0],),
        in_specs=[
            pl.BlockSpec((tm, dff), lambda g, *s: (s[_Items.IN_ROW][g], 0)),
            pl.BlockSpec(memory_space=pl.ANY),
            pl.BlockSpec((None, 1, tn), lambda g, *s: (s[E][g], 0, s[C][g])),
            pl.BlockSpec((tm, 1), lambda g, *s: (s[_Items.IN_ROW][g], 0)),
        ],
        out_specs=pl.BlockSpec((tm, tn), lambda g, *s: (s[R][g], s[C][g])),
        scratch_shapes=[pltpu.VMEM((2, 1, dff, tn), F32), pltpu.SemaphoreType.DMA((2, 1))],
    )
    return pl.pallas_call(
        functools.partial(_ffn2_kernel, tn=tn),
        grid_spec=grid_spec,
        out_shape=jax.ShapeDtypeStruct((n_rows, d), BF16),
        compiler_params=_cparams(("arbitrary",)),
        name="moe_ffn2",
    )(*items, act, w2, b2, row_gate)


def _residual_ln_kernel(h_ref, *refs):
    y_refs, (g_ref, b_ref, o_ref) = refs[:TOP_K], refs[TOP_K:]
    y = y_refs[0][...].astype(F32)
    for y_ref in y_refs[1:]:
        y = y + y_ref[...].astype(F32)
    o_ref[...] = _layer_norm(DEEPNORM_ALPHA * h_ref[...] + y, g_ref[...], b_ref[...])


def _residual_ln(h, ys, g, b, row0, m, tm):
    d = h.shape[1]
    tm = min(tm, m)
    off = row0 // tm
    tile = pl.BlockSpec((tm, d), lambda i: (i + off, 0))
    vec = pl.BlockSpec((1, d), lambda i: (0, 0))
    return pl.pallas_call(
        _residual_ln_kernel,
        grid=(m // tm,),
        in_specs=[tile] * (1 + TOP_K) + [vec, vec],
        out_specs=pl.BlockSpec((tm, d), lambda i: (i, 0)),
        out_shape=jax.ShapeDtypeStruct((m, d), F32),
        compiler_params=_cparams(("parallel",)),
        name="residual_ln2",
    )(h, *ys, g, b)


def _rel_bucket(rel):
    nb = REL_BUCKETS // 2
    max_exact = nb // 2
    ret = jnp.where(rel > 0, nb, 0)
    n = jnp.abs(rel)
    nf = jnp.maximum(n, 1).astype(F32)
    large = max_exact + (jnp.log(nf / max_exact) / math.log(REL_MAX_DIST / max_exact)
                         * (nb - max_exact)).astype(jnp.int32)
    large = jnp.minimum(large, nb - 1)
    return ret + jnp.where(n < max_exact, n, large)


def _bias_tables(rel_bias, tq):
    assert tq >= REL_MAX_DIST
    rel = 2 * tq - jnp.arange(4 * tq, dtype=jnp.int32)
    w = rel_bias[_rel_bucket(rel)].astype(F32).T[:, None, :]
    far = rel_bias[_rel_bucket(jnp.array([-(tq + 1), tq + 1], jnp.int32))].astype(F32)
    return w, far


def _rope_tables(seq):
    tok = jnp.arange(seq, dtype=jnp.int32)
    row_id = (tok // GRID_W).astype(F32)
    col_id = (tok % GRID_W).astype(F32)
    inv = ROPE_THETA ** (-jnp.arange(0, ROPE_AXIS_DIM, 2, dtype=F32) / ROPE_AXIS_DIM)
    ang_r = row_id[:, None] * inv[None, :]
    ang_c = col_id[:, None] * inv[None, :]
    cos = jnp.concatenate([jnp.cos(ang_r)] * 2 + [jnp.cos(ang_c)] * 2, axis=-1)
    sin = jnp.concatenate([-jnp.sin(ang_r), jnp.sin(ang_r), -jnp.sin(ang_c), jnp.sin(ang_c)], axis=-1)
    return cos, sin


def _route(logits, tm, n_tiles):
    n = logits.shape[0]
    top_val, top_idx = lax.top_k(logits, TOP_K)
    gate = jax.nn.softmax(top_val, axis=-1)
    n_assign = n * TOP_K
    e_flat = top_idx.reshape(-1)
    g_flat = gate.reshape(-1)
    experts = jnp.arange(N_EXPERTS, dtype=jnp.int32)
    counts = jnp.sum((experts[:, None] == e_flat[None, :]).astype(jnp.int32), axis=1)
    tiles_per = (counts + tm - 1) // tm
    tile_end = jnp.cumsum(tiles_per)
    pstart = (tile_end - tiles_per) * tm
    start = jnp.cumsum(counts) - counts
    order = jnp.argsort(e_flat).astype(jnp.int32)
    rank = jnp.argsort(order).astype(jnp.int32)
    dest = ((pstart - start)[e_flat] + rank).reshape(n, TOP_K)
    tiles = jnp.arange(n_tiles, dtype=jnp.int32)
    e_tile = jnp.minimum(jnp.sum((tile_end[None, :] <= tiles[:, None]).astype(jnp.int32), axis=1),
                         N_EXPERTS - 1)
    first = start[e_tile] + (tiles - (tile_end - tiles_per)[e_tile]) * tm
    last = (start + counts)[e_tile]
    a = first[:, None] + jnp.arange(tm, dtype=jnp.int32)[None, :]
    real = (a < last[:, None]).reshape(-1)
    src = order[jnp.clip(a, 0, n_assign - 1).reshape(-1)]
    row_tok = jnp.where(real, src // TOP_K, jnp.arange(n_tiles * tm, dtype=jnp.int32) % n)
    row_gate = jnp.where(real, g_flat[src], 0.0)
    return row_tok, row_gate, dest, tiles_per


PROJ_TM, PROJ_TN = 1024, 1024
PREP_TM = 512
ATTN_TQ = 512
GQA_HEADS_PER_STEP = 4
GQA_TQ = 1024
DIFF_TQ = 1024
OUT_TM, OUT_TK = 512, 512
LN_SLAB = 64
MOE_TM, MOE_TN = 512, 512
MOE_TN2 = 1024
LN_TM = 256


def kernel(x_prompt, x_sample, meta_tokens, rel_bias, w_in, lambda_q1, lambda_k1, lambda_q2,
           lambda_k2, a_subln, q_norm, k_norm, w_out, ln1_g, ln1_b, w_router, b_router,
           w1, b1, w2, b2, ln2_g, ln2_b):
    d = x_prompt.shape[-1]
    seq = x_prompt.shape[1]
    assert x_sample.shape[1] == seq and seq % GRID_W == 0
    nb_p, nb_s = x_prompt.shape[0], x_sample.shape[0]
    nb = nb_p + nb_s
    x_p, x_s = x_prompt.reshape(-1, d), x_sample.reshape(-1, d)
    n_p = x_p.shape[0]
    n = n_p + x_s.shape[0]
    x_b = jnp.concatenate([x_p.astype(BF16), x_s.astype(BF16)], axis=0)
    tq = min(ATTN_TQ, seq)

    w_in_b = w_in[0].astype(BF16)
    colscale = jnp.concatenate([jnp.full((A_WIDTH,), QUERY_SCALE, F32),
                                jnp.ones((IN_COLS - A_WIDTH,), F32)])[None]
    tn = min(PROJ_TN, IN_COLS)
    proj = _projection(x_b, w_in_b, colscale, PROJ_TM, tn)
    meta_rows = jnp.zeros((META_PAD, d), F32).at[:N_META].set(meta_tokens)
    meta_proj = _projection(meta_rows.astype(BF16), w_in_b, colscale, META_PAD, tn)

    gains = jnp.concatenate([jnp.tile(q_norm[0] * QUERY_SCALE, B_HEADS),
                             jnp.tile(k_norm[0], B_KV_HEADS)]).reshape(-1, 1, B_KV_WIDTH)
    cos, sin = _rope_tables(seq)
    bqk = _qk_prep(proj, gains, cos, sin, min(PREP_TM, seq))
    meta_bqk = _qk_prep(meta_proj, gains, jnp.ones((META_PAD, HEAD_DIM), F32),
                        jnp.zeros((META_PAD, HEAD_DIM), F32), META_PAD)

    lam = (jnp.exp(jnp.sum(lambda_q1[0].astype(F32) * lambda_k1[0].astype(F32)))
           - jnp.exp(jnp.sum(lambda_q2[0].astype(F32) * lambda_k2[0].astype(F32))) + LAMBDA_INIT)
    w_bias, far = _bias_tables(rel_bias * LOG2E, tq)
    consts = jnp.concatenate([far[0], far[1], lam[None]]).astype(F32)
    proj3 = proj.reshape(nb, seq, IN_COLS)
    a_v0, b_v0 = 2 * A_WIDTH, 3 * A_WIDTH + B_WIDTH + B_KV_WIDTH
    vt_a, meta_vt_a = _chunked_transpose(proj3[:, :, a_v0:a_v0 + A_WIDTH],
                                         meta_proj[:, a_v0:a_v0 + A_WIDTH], A_HEADS, 2 * HEAD_DIM, tq)
    vt_b, meta_vt_b = _chunked_transpose(proj3[:, :, b_v0:], meta_proj[:, b_v0:],
                                         B_KV_HEADS, HEAD_DIM, tq)
    a_out = _diff_attention(proj3, vt_a, meta_proj, meta_vt_a, w_bias, consts,
                            a_subln.astype(F32).reshape(-1, 1), min(DIFF_TQ, seq), tq)
    b_out = _gqa_attention(bqk.reshape(nb, seq, -1), vt_b, meta_bqk, meta_vt_b,
                           min(GQA_TQ, seq), tq, GQA_HEADS_PER_STEP)

    a2, b2d, w_out_b = a_out.reshape(n, A_WIDTH), b_out.reshape(n, B_WIDTH), w_out[0].astype(BF16)
    h1, h1_b, logits = _out_proj_ln(a2, b2d, w_out_b, x_p, x_s, ln1_g, ln1_b,
                                    w_router[0], b_router, OUT_TM, OUT_TK)

    tm = min(MOE_TM, n)
    n_tiles = (n * TOP_K + N_EXPERTS * (tm - 1)) // tm
    row_tok, row_gate, dest, tiles_per = _route(logits, tm, n_tiles)
    x_rows = h1_b[row_tok]
    dff = w2.shape[2]
    tn1, tn2 = min(MOE_TN, dff), min(MOE_TN2, d)
    act = _expert_ffn1(x_rows, w1[0], b1[0].reshape(N_EXPERTS, 1, -1),
                       _work_items(tiles_per, dff // tn1, n_tiles * (dff // tn1)), tm, tn1)
    y_rows = _expert_ffn2(act, w2[0], b2[0].reshape(N_EXPERTS, 1, -1), row_gate[:, None],
                          _work_items(tiles_per, d // tn2, n_tiles * (d // tn2)), tm, tn2)
    ys = [y_rows[dest[:, k]] for k in range(TOP_K)]

    out_p = _residual_ln(h1, ys, ln2_g, ln2_b, 0, n_p, LN_TM)
    out_s = _residual_ln(h1, ys, ln2_g, ln2_b, n_p, n - n_p, LN_TM)
    return (out_p.reshape(nb_p, seq, d), out_s.reshape(nb_s, seq, d))
```
